```python
import math
import jax
import jax.numpy as jnp
from jax import lax
import numpy as np

D_MODEL = 1024
BATCH = 8
SEQ = 4096
DEPTH = 1
DEC_BATCH = 32
DEC_SEQ = 8
PAST_LEN = 16384
PAGE_SIZE = 128

N_META = 16
D_QK = 64
N_HEADS_A = D_MODEL // (2 * D_QK)
D_VA = 2 * D_QK
ROPE_DIM = D_QK // 4
ROPE_THETA = 500000.0
Q_BLOCK = 128
N_HEADS_M = 4
D_M = D_MODEL
D_HM = D_M // N_HEADS_M
CONV_W = 4
CHUNK = 64
D_AQK = N_HEADS_A * 2 * D_QK
D_A = N_HEADS_A * D_VA
IN_SIZES = (D_AQK, D_AQK, D_A, 2 * D_M, D_M, D_M, N_HEADS_M, N_HEADS_M, D_MODEL, D_MODEL)
D_IN = 2 * D_AQK + D_A + 4 * D_M + 2 * N_HEADS_M + 2 * D_MODEL
N_EXP = 256
TOP_K = 8
D_EXP = D_MODEL // 4
N_GROUPS = 8
TOPK_GROUPS = 4
ROUTE_SCALE = 2.5
MOE_BLOCK = 128
ALPHA = (2 * DEPTH) ** 0.25
BETA = (8 * DEPTH) ** -0.25
EPS = 1e-5
F32 = jnp.float32

kernel_name = 'hybrid_diffattn_mlstm_moe_step'


def layer_norm(x, g, b):
    xf = x.astype(F32)
    mu = jnp.mean(xf, -1, keepdims=True)
    var = jnp.mean(jnp.square(xf - mu), -1, keepdims=True)
    return ((xf - mu) * lax.rsqrt(var + EPS) * g.astype(F32) + b.astype(F32)).astype(x.dtype)


def rms_norm(x, g):
    xf = x.astype(F32)
    return xf * lax.rsqrt(jnp.mean(jnp.square(xf), -1, keepdims=True) + EPS) * g.astype(F32)


def split_in(z):
    cuts, o = [], 0
    for s in IN_SIZES[:-1]:
        o += s
        cuts.append(o)
    return jnp.split(z, cuts, axis=-1)


def rope(x, pos):
    half = ROPE_DIM // 2
    inv = ROPE_THETA ** (-jnp.arange(half, dtype=F32) * 2.0 / ROPE_DIM)
    ang = pos.astype(F32)[:, None] * inv
    cos = jnp.cos(ang)[:, None, None, :]
    sin = jnp.sin(ang)[:, None, None, :]
    xf = x.astype(F32)
    x1, x2 = xf[..., :half], xf[..., half:ROPE_DIM]
    out = jnp.concatenate([x1 * cos - x2 * sin, x2 * cos + x1 * sin, xf[..., ROPE_DIM:]], axis=-1)
    return out.astype(x.dtype)


def project(x, w_in, b_in, pos):
    B, L, _ = x.shape
    z = jnp.einsum('bld,de->ble', x, w_in) + b_in
    qa, ka, va, qkm, vm, om, im, fm, ga, gm = split_in(z)
    qa = rope(qa.reshape(B, L, N_HEADS_A, 2, D_QK), pos)
    ka = rope(ka.reshape(B, L, N_HEADS_A, 2, D_QK), pos)
    va = va.reshape(B, L, N_HEADS_A, D_VA)
    return qa, ka, va, qkm, vm, om, im, fm, ga, gm


def diff_lambda(lq1, lk1, lq2, lk2, lam_init):
    return (jnp.exp(jnp.sum(lq1.astype(F32) * lk1.astype(F32)))
            - jnp.exp(jnp.sum(lq2.astype(F32) * lk2.astype(F32))) + lam_init)


def diff_attn_prompt(q, k, v, lam):
    B, L = q.shape[:2]
    n_blk = -(-L // Q_BLOCK)
    qp = jnp.pad(q, ((0, 0), (0, n_blk * Q_BLOCK - L), (0, 0), (0, 0), (0, 0)))
    qb = jnp.moveaxis(qp.reshape((B, n_blk, Q_BLOCK) + q.shape[2:]), 1, 0)
    k_pos = jnp.arange(L)
    vf = v.astype(F32)
    scale = D_QK ** -0.5

    def one_block(args):
        q_blk, i = args
        q_pos = i * Q_BLOCK + jnp.arange(Q_BLOCK)
        s = jnp.einsum('bqhmd,bkhmd->bhmqk', q_blk, k, preferred_element_type=F32) * scale
        s = jnp.where(k_pos[None, :] <= q_pos[:, None], s, -jnp.inf)
        p = jax.nn.softmax(s, axis=-1)
        a = p[:, :, 0] - lam * p[:, :, 1]
        return jnp.einsum('bhqk,bkhd->bqhd', a, vf)

    o = lax.map(one_block, (qb, jnp.arange(n_blk)))
    return jnp.moveaxis(o, 0, 1).reshape(B, n_blk * Q_BLOCK, N_HEADS_A, D_VA)[:, :L]


def diff_attn_sample(q, k, v, cache_k, cache_v, layer, page_table, lam):
    Bd, S = q.shape[:2]
    qs = q.astype(F32) * (D_QK ** -0.5)

    def update(carry, s, vb):
        m, l, acc = carry
        m_new = jnp.maximum(m, jnp.max(s, axis=-1))
        corr = jnp.exp(m - m_new)
        p = jnp.exp(s - m_new[..., None])
        l = l * corr + jnp.sum(p, axis=-1)
        acc = acc * corr[..., None] + jnp.einsum('bhmqk,bkhd->bhmqd', p, vb.astype(F32))
        return (m_new, l, acc)

    def page_step(carry, pidx):
        kp = cache_k[layer, pidx].reshape(Bd, PAGE_SIZE, N_HEADS_A, 2, D_QK)
        s = jnp.einsum('bqhmd,bkhmd->bhmqk', qs, kp.astype(F32))
        return update(carry, s, cache_v[layer, pidx]), None

    init = (jnp.full((Bd, N_HEADS_A, 2, S), -jnp.inf, F32),
            jnp.zeros((Bd, N_HEADS_A, 2, S), F32),
            jnp.zeros((Bd, N_HEADS_A, 2, S, D_VA), F32))
    carry, _ = lax.scan(page_step, init, page_table.T)
    s_new = jnp.einsum('bqhmd,bkhmd->bhmqk', qs, k.astype(F32))
    s_new = jnp.where(jnp.tril(jnp.ones((S, S), dtype=bool)), s_new, -jnp.inf)
    m, l, acc = update(carry, s_new, v)
    o = acc / l[..., None]
    out = o[:, :, 0] - lam * o[:, :, 1]
    return jnp.transpose(out, (0, 2, 1, 3))


def diff_post(o, g, lam_init, dtype):
    B, L = o.shape[:2]
    return (rms_norm(o, g) * (1.0 - lam_init)).reshape(B, L, D_A).astype(dtype)


def causal_dwconv(xpad, w, b):
    y = lax.conv_general_dilated(xpad, w[:, None, :].astype(xpad.dtype), window_strides=(1,),
                                 padding='VALID', dimension_numbers=('NWC', 'WIO', 'NWC'),
                                 feature_group_count=xpad.shape[-1])
    return jax.nn.silu(y + b)


def mlstm_inputs(qk, vm, im, fm):
    B, L, _ = vm.shape
    q = qk[..., :D_M].reshape(B, L, N_HEADS_M, D_HM).astype(F32)
    k = qk[..., D_M:].reshape(B, L, N_HEADS_M, D_HM).astype(F32) * (D_HM ** -0.5)
    v = vm.reshape(B, L, N_HEADS_M, D_HM).astype(F32)
    return q, k, v, im.astype(F32), jax.nn.log_sigmoid(fm.astype(F32))


def mlstm_chunk(state, inp):
    C, n, m = state
    q, k, v, li, lf = inp
    T = q.shape[1]
    b = jnp.cumsum(lf, axis=1)
    causal = jnp.tril(jnp.ones((T, T), dtype=bool))
    log_d = jnp.where(causal[None, :, :, None],
                      b[:, :, None, :] - b[:, None, :, :] + li[:, None, :, :], -jnp.inf)
    log_inter = b + m[:, None, :]
    m_t = jnp.maximum(log_inter, jnp.max(log_d, axis=2))
    s = jnp.einsum('bthd,bshd->btsh', q, k) * jnp.exp(log_d - m_t[:, :, None, :])
    w_inter = jnp.exp(log_inter - m_t)
    num = jnp.einsum('btsh,bshd->bthd', s, v) + w_inter[..., None] * jnp.einsum('bhvk,bthk->bthv', C, q)
    den = jnp.sum(s, axis=2) + w_inter * jnp.einsum('bhk,bthk->bth', n, q)
    h = num / jnp.maximum(jnp.abs(den), jnp.exp(-m_t))[..., None]
    b_last = b[:, -1]
    log_w = b_last[:, None, :] - b + li
    m_new = jnp.maximum(b_last + m, jnp.max(log_w, axis=1))
    w = jnp.exp(log_w - m_new[:, None, :])
    decay = jnp.exp(b_last + m - m_new)
    C_new = decay[..., None, None] * C + jnp.einsum('bshv,bshk->bhvk', w[..., None] * v, k)
    n_new = decay[..., None] * n + jnp.einsum('bsh,bshk->bhk', w, k)
    return (C_new, n_new, m_new), h


def mlstm_prompt(q, k, v, li, lf):
    B, L = q.shape[:2]
    n_ch = (L - N_META) // CHUNK
    st0 = (jnp.zeros((B, N_HEADS_M, D_HM, D_HM), F32), jnp.zeros((B, N_HEADS_M, D_HM), F32),
           jnp.zeros((B, N_HEADS_M), F32))
    inps = (q, k, v, li, lf)
    st, h_meta = mlstm_chunk(st0, tuple(a[:, :N_META] for a in inps))
    rest = tuple(jnp.moveaxis(a[:, N_META:].reshape((B, n_ch, CHUNK) + a.shape[2:]), 1, 0) for a in inps)
    st, h_rest = lax.scan(mlstm_chunk, st, rest)
    h_rest = jnp.moveaxis(h_rest, 0, 1).reshape(B, L - N_META, N_HEADS_M, D_HM)
    return jnp.concatenate([h_meta, h_rest], axis=1), st


def mlstm_post(h, om, g, dtype):
    B, L = h.shape[:2]
    hn = rms_norm(h, g.reshape(N_HEADS_M, D_HM)).reshape(B, L, D_M)
    return (jax.nn.sigmoid(om.astype(F32)) * hn).astype(dtype)


def swiglu(x, w1, w3, w2):
    return (jax.nn.silu(x @ w1) * (x @ w3)) @ w2


def routed_experts(x2d, idx, g, w1, w3, w2):
    T, D = x2d.shape
    TK = T * TOP_K
    blk = max(8, min(MOE_BLOCK, TK // N_EXP))
    n_blk = -(-TK // blk) + N_EXP
    flat_e = idx.reshape(TK)
    order = jnp.argsort(flat_e)
    e_sorted = flat_e[order]
    tok_sorted = (order // TOP_K).astype(jnp.int32)
    g_sorted = g.reshape(TK)[order]
    counts = jnp.bincount(flat_e, length=N_EXP)
    padded = (counts + blk - 1) // blk * blk
    pad_end = jnp.cumsum(padded)
    start = jnp.cumsum(counts) - counts
    dest = pad_end[e_sorted] - padded[e_sorted] + jnp.arange(TK) - start[e_sorted]
    rows = jnp.zeros((n_blk * blk,), jnp.int32).at[dest].set(tok_sorted)
    wrow = jnp.zeros((n_blk * blk,), F32).at[dest].set(g_sorted)
    blk_e = jnp.minimum(jnp.searchsorted(pad_end, jnp.arange(n_blk) * blk, side='right'), N_EXP - 1)

    def step(y, inp):
        r, wr, e = inp
        xb = x2d[r]
        out = swiglu(xb, w1[e], w3[e], w2[e]).astype(F32) * wr[:, None]
        return y.at[r].add(out), None

    y, _ = lax.scan(step, jnp.zeros((T, D), F32),
                    (rows.reshape(n_blk, blk), wrow.reshape(n_blk, blk), blk_e))
    return y


def moe_ffn(x2d, w_router, b_router, w1, w3, w2, ws1, ws3, ws2):
    T = x2d.shape[0]
    s = jax.nn.sigmoid(jnp.einsum('td,de->te', x2d, w_router, preferred_element_type=F32))
    s_sel = s + b_router.astype(F32)
    grp = lax.top_k(s_sel.reshape(T, N_GROUPS, N_EXP // N_GROUPS), 2)[0].sum(-1)
    _, gidx = lax.top_k(grp, TOPK_GROUPS)
    gmask = jnp.any(gidx[..., None] == jnp.arange(N_GROUPS), axis=1)
    emask = jnp.repeat(gmask, N_EXP // N_GROUPS, axis=1)
    _, idx = lax.top_k(jnp.where(emask, s_sel, -jnp.inf), TOP_K)
    g = jnp.take_along_axis(s, idx, axis=1)
    g = g / jnp.sum(g, -1, keepdims=True) * ROUTE_SCALE
    y = routed_experts(x2d, idx, g, w1, w3, w2) + swiglu(x2d, ws1, ws3, ws2).astype(F32)
    return y.astype(x2d.dtype)


def merge_and_channel_mix(x, ya, ym, ga, gm, w_ba, w_bm, w_out, ln1_g, ln1_b, w_router, b_router,
                          w1, w3, w2, ws1, ws3, ws2, ln2_g, ln2_b):
    mix = (jax.nn.sigmoid(ga) * (ya @ w_ba) + jax.nn.sigmoid(gm) * (ym @ w_bm)) @ w_out
    h = layer_norm(ALPHA * x + mix, ln1_g, ln1_b)
    B, L, D = h.shape
    f = moe_ffn(h.reshape(B * L, D), w_router, b_router, w1, w3, w2, ws1, ws3, ws2).reshape(B, L, D)
    return layer_norm(ALPHA * h + f, ln2_g, ln2_b)


def setup_inputs(seed: int = 0) -> dict:
    key = jax.random.key(seed)
    ks = jax.random.split(key, 36)

    def nrm(i, shape, scale=1.0):
        return jax.random.normal(ks[i], shape, F32) * scale

    def gain(i, shape):
        return 1.0 + nrm(i, shape, 0.02)

    n_pages = PAST_LEN // PAGE_SIZE
    n_pool = (DEC_BATCH * n_pages * 5) // 4
    page_table = jax.random.permutation(ks[4], n_pool)[: DEC_BATCH * n_pages].reshape(DEC_BATCH, n_pages).astype(jnp.int32)
    f_off = 2 * D_AQK + D_A + 4 * D_M + N_HEADS_M
    b_in = nrm(10, (DEPTH, D_IN), 0.02).at[:, f_off:f_off + N_HEADS_M].add(jnp.linspace(3.0, 6.0, N_HEADS_M))
    return {
        'x_prompt': nrm(0, (BATCH, SEQ, D_MODEL)),
        'x_sample': nrm(1, (DEC_BATCH, DEC_SEQ, D_MODEL)),
        'cache_k': nrm(2, (DEPTH, n_pool, PAGE_SIZE, N_HEADS_A, 2 * D_QK)),
        'cache_v': nrm(3, (DEPTH, n_pool, PAGE_SIZE, N_HEADS_A, D_VA)),
        'page_table': page_table,
        'state_C': nrm(5, (DEPTH, DEC_BATCH, N_HEADS_M, D_HM, D_HM), 0.1),
        'state_n': nrm(6, (DEPTH, DEC_BATCH, N_HEADS_M, D_HM), 0.1),
        'state_m': nrm(7, (DEPTH, DEC_BATCH, N_HEADS_M)),
        'state_conv': nrm(8, (DEPTH, DEC_BATCH, CONV_W - 1, 2 * D_M)),
        'meta_tokens': nrm(9, (N_META, D_MODEL)),
        'w_in': nrm(11, (DEPTH, D_MODEL, D_IN), D_MODEL ** -0.5),
        'b_in': b_in,
        'lam_q1': nrm(12, (DEPTH, D_QK), 0.1),
        'lam_k1': nrm(13, (DEPTH, D_QK), 0.1),
        'lam_q2': nrm(14, (DEPTH, D_QK), 0.1),
        'lam_k2': nrm(15, (DEPTH, D_QK), 0.1),
        'g_subln': gain(16, (DEPTH, D_VA)),
        'conv_w': nrm(17, (DEPTH, CONV_W, 2 * D_M), CONV_W ** -0.5),
        'conv_b': nrm(18, (DEPTH, 2 * D_M), 0.02),
        'g_mnorm': gain(19, (DEPTH, D_M)),
        'w_ba': nrm(20, (DEPTH, D_A, D_MODEL), D_A ** -0.5),
        'w_bm': nrm(21, (DEPTH, D_M, D_MODEL), D_M ** -0.5),
        'w_out': nrm(22, (DEPTH, D_MODEL, D_MODEL), BETA * D_MODEL ** -0.5),
        'ln1_g': gain(23, (DEPTH, D_MODEL)),
        'ln1_b': nrm(24, (DEPTH, D_MODEL), 0.02),
        'w_router': nrm(25, (DEPTH, D_MODEL, N_EXP), D_MODEL ** -0.5),
        'b_router': nrm(26, (DEPTH, N_EXP), 0.01),
        'w1': nrm(27, (DEPTH, N_EXP, D_MODEL, D_EXP), D_MODEL ** -0.5),
        'w3': nrm(28, (DEPTH, N_EXP, D_MODEL, D_EXP), D_MODEL ** -0.5),
        'w2': nrm(29, (DEPTH, N_EXP, D_EXP, D_MODEL), BETA * D_EXP ** -0.5),
        'ws1': nrm(30, (DEPTH, D_MODEL, D_EXP), D_MODEL ** -0.5),
        'ws3': nrm(31, (DEPTH, D_MODEL, D_EXP), D_MODEL ** -0.5),
        'ws2': nrm(32, (DEPTH, D_EXP, D_MODEL), BETA * D_EXP ** -0.5),
        'ln2_g': gain(33, (DEPTH, D_MODEL)),
        'ln2_b': nrm(34, (DEPTH, D_MODEL), 0.02),
    }


def reference(x_prompt, x_sample, cache_k, cache_v, page_table, state_C, state_n, state_m, state_conv,
              meta_tokens, w_in, b_in, lam_q1, lam_k1, lam_q2, lam_k2, g_subln, conv_w, conv_b, g_mnorm,
              w_ba, w_bm, w_out, ln1_g, ln1_b, w_router, b_router, w1, w3, w2, ws1, ws3, ws2, ln2_g, ln2_b):
    B = x_prompt.shape[0]
    dt = x_prompt.dtype
    xp = jnp.concatenate([jnp.broadcast_to(meta_tokens.astype(dt)[None], (B, N_META, D_MODEL)), x_prompt], axis=1)
    xs = x_sample
    L = xp.shape[1]
    S = xs.shape[1]
    pos_p = jnp.arange(L)
    pos_s = page_table.shape[1] * PAGE_SIZE + jnp.arange(S)
    kp_l, vp_l, Cp_l, np_l, mp_l, cp_l = [], [], [], [], [], []
    ks_l, vs_l, Cs_l, ns_l, ms_l, cs_l = [], [], [], [], [], []
    for l in range(DEPTH):
        lam_init = 0.8 - 0.6 * math.exp(-0.3 * l)
        lam = diff_lambda(lam_q1[l], lam_k1[l], lam_q2[l], lam_k2[l], lam_init)
        ffn_args = (w_ba[l], w_bm[l], w_out[l], ln1_g[l], ln1_b[l], w_router[l], b_router[l],
                    w1[l], w3[l], w2[l], ws1[l], ws3[l], ws2[l], ln2_g[l], ln2_b[l])
        qa, ka, va, qkm, vm, om, im, fm, ga, gm = project(xp, w_in[l], b_in[l], pos_p)
        ya = diff_post(diff_attn_prompt(qa, ka, va, lam), g_subln[l], lam_init, dt)
        qk_pad = jnp.pad(qkm, ((0, 0), (CONV_W - 1, 0), (0, 0)))
        hm, (C, n, m) = mlstm_prompt(*mlstm_inputs(causal_dwconv(qk_pad, conv_w[l], conv_b[l]), vm, im, fm))
        ym = mlstm_post(hm, om, g_mnorm[l], dt)
        kp_l.append(ka.reshape(B, L, N_HEADS_A, 2 * D_QK))
        vp_l.append(va)
        Cp_l.append(C.astype(state_C.dtype))
        np_l.append(n.astype(state_n.dtype))
        mp_l.append(m.astype(state_m.dtype))
        cp_l.append(qk_pad[:, L:])
        xp = merge_and_channel_mix(xp, ya, ym, ga, gm, *ffn_args)
        qa, ka, va, qkm, vm, om, im, fm, ga, gm = project(xs, w_in[l], b_in[l], pos_s)
        ya = diff_post(diff_attn_sample(qa, ka, va, cache_k, cache_v, l, page_table, lam), g_subln[l], lam_init, xs.dtype)
        qk_pad = jnp.concatenate([state_conv[l].astype(qkm.dtype), qkm], axis=1)
        st = (state_C[l].astype(F32), state_n[l].astype(F32), state_m[l].astype(F32))
        (C, n, m), hm = mlstm_chunk(st, mlstm_inputs(causal_dwconv(qk_pad, conv_w[l], conv_b[l]), vm, im, fm))
        ym = mlstm_post(hm, om, g_mnorm[l], xs.dtype)
        ks_l.append(ka.reshape(xs.shape[0], S, N_HEADS_A, 2 * D_QK))
        vs_l.append(va)
        Cs_l.append(C.astype(state_C.dtype))
        ns_l.append(n.astype(state_n.dtype))
        ms_l.append(m.astype(state_m.dtype))
        cs_l.append(qk_pad[:, S:])
        xs = merge_and_channel_mix(xs, ya, ym, ga, gm, *ffn_args)
    return (xp[:, N_META:], xs,
            jnp.stack(kp_l), jnp.stack(vp_l), jnp.stack(Cp_l), jnp.stack(np_l), jnp.stack(mp_l), jnp.stack(cp_l),
            jnp.stack(ks_l), jnp.stack(vs_l), jnp.stack(Cs_l), jnp.stack(ns_l), jnp.stack(ms_l), jnp.stack(cs_l))
```

```python
import functools
import math

import jax
import jax.numpy as jnp
from jax import lax
from jax.experimental import pallas as pl
from jax.experimental.pallas import tpu as pltpu

F32 = jnp.float32
BF16 = jnp.bfloat16

DEPTH = 1
N_META = 16
D_QK = 64
ROPE_DIM = D_QK // 4
ROPE_THETA = 500000.0
CONV_W = 4
TOP_K = 8
N_GROUPS = 8
TOPK_GROUPS = 4
ROUTE_SCALE = 2.5
ALPHA = (2 * DEPTH) ** 0.25
EPS = 1e-5
NEG = -1e30

LANE = 128
VMEM_LIMIT = 52 * 1024 * 1024

ATTN_TILE = 512
MLSTM_CHUNK = 256
PAD_CHUNK = 128
PAGES_PER_STEP = 8
MOE_BLOCK = 256


def _cparams(*sem):
    return pltpu.CompilerParams(dimension_semantics=sem, vmem_limit_bytes=VMEM_LIMIT)


def _dot(a, b):
    return jnp.dot(a, b, preferred_element_type=F32)


def _dot_nt(a, b):
    return lax.dot_general(a, b, (((1,), (1,)), ((), ())), preferred_element_type=F32)


def _dot_tn(a, b):
    return lax.dot_general(a, b, (((0,), (0,)), ((), ())), preferred_element_type=F32)


def _split3(a):
    a1 = a.astype(BF16)
    r1 = a - a1.astype(F32)
    a2 = r1.astype(BF16)
    a3 = (r1 - a2.astype(F32)).astype(BF16)
    return a1, a2, a3


def _sigmoid(x):
    return 1.0 / (1.0 + jnp.exp(-x))


def _qkv_kernel(x_ref, w_ref, b_ref, cos_ref, sin_ref,
                qlo_ref, qhi_ref, kf_ref, kb_ref, vf_ref, vb_ref, *, n_heads, d_model):
    x = x_ref[...]
    cos = cos_ref[...]
    sin = sin_ref[...]
    lane = lax.broadcasted_iota(jnp.int32, cos.shape, 1)
    first = (lane % D_QK) < (ROPE_DIM // 2)
    lo_half = lane < D_QK
    half = ROPE_DIM // 2

    def rope(s):
        up = pltpu.roll(s, LANE - half, axis=1)
        dn = pltpu.roll(s, half, axis=1)
        return s * cos + jnp.where(first, up, dn) * sin

    zq = _dot(x, w_ref[:, 0:d_model]) + b_ref[:, 0:d_model]
    for h in range(n_heads):
        q = rope(zq[:, h * LANE:(h + 1) * LANE]) * (D_QK ** -0.5)
        qlo_ref[h] = jnp.where(lo_half, q, 0.0).astype(BF16)
        qhi_ref[h] = jnp.where(lo_half, 0.0, q).astype(BF16)
    zk = _dot(x, w_ref[:, d_model:2 * d_model]) + b_ref[:, d_model:2 * d_model]
    for h in range(n_heads):
        k = rope(zk[:, h * LANE:(h + 1) * LANE])
        kf_ref[:, h * LANE:(h + 1) * LANE] = k
        kb_ref[h] = k.astype(BF16)
    zv = _dot(x, w_ref[:, 2 * d_model:3 * d_model]) + b_ref[:, 2 * d_model:3 * d_model]
    vf_ref[...] = zv
    for h in range(n_heads):
        vb_ref[h] = zv[:, h * LANE:(h + 1) * LANE].astype(BF16)


def _proj_qkv(x_bf, w_bf, bias, cos, sin, tm, n_heads):
    m, d = x_bf.shape
    n_pos = cos.shape[0] // tm
    hm = jax.ShapeDtypeStruct((n_heads, m, LANE), BF16)
    fm = jax.ShapeDtypeStruct((m, d), F32)
    hspec = pl.BlockSpec((n_heads, tm, LANE), lambda i: (0, i, 0))
    fspec = pl.BlockSpec((tm, d), lambda i: (i, 0))
    pspec = pl.BlockSpec((tm, LANE), lambda i: (i % n_pos, 0))
    return pl.pallas_call(
        functools.partial(_qkv_kernel, n_heads=n_heads, d_model=d),
        out_shape=(hm, hm, fm, hm, fm, hm),
        grid=(m // tm,),
        in_specs=[fspec, pl.BlockSpec((d, 3 * d), lambda i: (0, 0)),
                  pl.BlockSpec((1, 3 * d), lambda i: (0, 0)), pspec, pspec],
        out_specs=(hspec, hspec, fspec, hspec, fspec, hspec),
        compiler_params=_cparams("parallel"),
        name="proj_qkv",
    )(x_bf, w_bf, bias, cos, sin)


def _mix_proj_kernel(x_ref, w_ref, b_ref, qk_ref, v_ref, om_ref, *, d_model):
    x = x_ref[...]
    d = d_model
    qk_ref[...] = _dot(x, w_ref[:, 0:2 * d]) + b_ref[:, 0:2 * d]
    v_ref[...] = (_dot(x, w_ref[:, 2 * d:3 * d]) + b_ref[:, 2 * d:3 * d]).astype(BF16)
    om_ref[...] = (_dot(x, w_ref[:, 3 * d:4 * d]) + b_ref[:, 3 * d:4 * d]).astype(BF16)


def _proj_mix(x_bf, w_bf, bias, tm):
    m, d = x_bf.shape
    return pl.pallas_call(
        functools.partial(_mix_proj_kernel, d_model=d),
        out_shape=(jax.ShapeDtypeStruct((m, 2 * d), F32), jax.ShapeDtypeStruct((m, d), BF16),
                   jax.ShapeDtypeStruct((m, d), BF16)),
        grid=(m // tm,),
        in_specs=[pl.BlockSpec((tm, d), lambda i: (i, 0)),
                  pl.BlockSpec((d, 4 * d), lambda i: (0, 0)),
                  pl.BlockSpec((1, 4 * d), lambda i: (0, 0))],
        out_specs=(pl.BlockSpec((tm, 2 * d), lambda i: (i, 0)), pl.BlockSpec((tm, d), lambda i: (i, 0)),
                   pl.BlockSpec((tm, d), lambda i: (i, 0))),
        compiler_params=_cparams("parallel"),
        name="proj_mix",
    )(x_bf, w_bf, bias)


def _gate_proj_kernel(x_ref, w_ref, b_ref, ga_ref, gm_ref, *, d_model):
    x = x_ref[...]
    d = d_model
    ga_ref[...] = _sigmoid(_dot(x, w_ref[:, 0:d]) + b_ref[:, 0:d]).astype(BF16)
    gm_ref[...] = _sigmoid(_dot(x, w_ref[:, d:2 * d]) + b_ref[:, d:2 * d]).astype(BF16)


def _proj_gates(x_bf, w_bf, bias, tm):
    m, d = x_bf.shape
    return pl.pallas_call(
        functools.partial(_gate_proj_kernel, d_model=d),
        out_shape=(jax.ShapeDtypeStruct((m, d), BF16), jax.ShapeDtypeStruct((m, d), BF16)),
        grid=(m // tm,),
        in_specs=[pl.BlockSpec((tm, d), lambda i: (i, 0)),
                  pl.BlockSpec((d, 2 * d), lambda i: (0, 0)),
                  pl.BlockSpec((1, 2 * d), lambda i: (0, 0))],
        out_specs=(pl.BlockSpec((tm, d), lambda i: (i, 0)), pl.BlockSpec((tm, d), lambda i: (i, 0))),
        compiler_params=_cparams("parallel"),
        name="proj_gates",
    )(x_bf, w_bf, bias)


def _if_proj_kernel(x_ref, wh_ref, wl_ref, b_ref, o_ref, *, n_heads):
    x = x_ref[...]
    xh = x.astype(BF16)
    xl = (x - xh.astype(F32)).astype(BF16)
    wh = wh_ref[...]
    z = _dot(xh, wh) + _dot(xl, wh) + _dot(xh, wl_ref[...]) + b_ref[...]
    lane = lax.broadcasted_iota(jnp.int32, z.shape, 1)
    logsig = jnp.minimum(z, 0.0) - jnp.log(1.0 + jnp.exp(-jnp.abs(z)))
    o_ref[...] = jnp.where(lane < n_heads, z, logsig)


def _proj_if(x_f32, w_if, b_if, tm, n_heads):
    m, d = x_f32.shape
    wpad = jnp.zeros((d, LANE), F32).at[:, :2 * n_heads].set(w_if)
    bpad = jnp.zeros((1, LANE), F32).at[0, :2 * n_heads].set(b_if)
    wh = wpad.astype(BF16)
    wl = (wpad - wh.astype(F32)).astype(BF16)
    return pl.pallas_call(
        functools.partial(_if_proj_kernel, n_heads=n_heads),
        out_shape=jax.ShapeDtypeStruct((m, LANE), F32),
        grid=(m // tm,),
        in_specs=[pl.BlockSpec((tm, d), lambda i: (i, 0)),
                  pl.BlockSpec((d, LANE), lambda i: (0, 0)),
                  pl.BlockSpec((d, LANE), lambda i: (0, 0)),
                  pl.BlockSpec((1, LANE), lambda i: (0, 0))],
        out_specs=pl.BlockSpec((tm, LANE), lambda i: (i, 0)),
        compiler_params=_cparams("parallel"),
        name="proj_if",
    )(x_f32, wh, wl, bpad)


def _flash_update(idx, q, k, v, mask, m_ref, l_ref, acc_ref):
    s = _dot_nt(q, k)
    if mask is not None:
        s = jnp.where(mask, s, NEG)
    reps = s.shape[1] // LANE
    m_prev = m_ref[idx]
    m_new = jnp.maximum(m_prev, jnp.max(s, axis=1, keepdims=True))
    alpha = jnp.exp(m_prev - m_new)
    p = jnp.exp(s - jnp.tile(m_new, (1, reps)))
    l_ref[idx] = alpha * l_ref[idx] + jnp.sum(p, axis=1, keepdims=True)
    acc_ref[idx] = alpha * acc_ref[idx] + _dot(p.astype(BF16), v)
    m_ref[idx] = m_new


def _attn_kernel(qi_tab, kj_tab, qlo_ref, qhi_ref, k_ref, v_ref, km_ref, vm_ref, lam_ref, g_ref,
                 o_ref, m_ref, l_ref, acc_ref, *, n_heads, lam_init):
    s_idx = pl.program_id(1)
    qi = qi_tab[s_idx]
    kj = kj_tab[s_idx]
    tq = qlo_ref.shape[1]
    tk = k_ref.shape[1]

    @pl.when(kj < 0)
    def _meta():
        m_ref[...] = jnp.full(m_ref.shape, NEG, F32)
        l_ref[...] = jnp.zeros(l_ref.shape, F32)
        acc_ref[...] = jnp.zeros(acc_ref.shape, F32)
        mask = lax.broadcasted_iota(jnp.int32, (tq, km_ref.shape[1]), 1) < N_META

        def body(h, c):
            _flash_update(2 * h, qlo_ref[h], km_ref[h], vm_ref[h], mask, m_ref, l_ref, acc_ref)
            _flash_update(2 * h + 1, qhi_ref[h], km_ref[h], vm_ref[h], mask, m_ref, l_ref, acc_ref)
            return c
        lax.fori_loop(0, n_heads, body, 0)

    @pl.when(jnp.logical_and(kj >= 0, kj < qi))
    def _full():
        def body(h, c):
            _flash_update(2 * h, qlo_ref[h], k_ref[h], v_ref[h], None, m_ref, l_ref, acc_ref)
            _flash_update(2 * h + 1, qhi_ref[h], k_ref[h], v_ref[h], None, m_ref, l_ref, acc_ref)
            return c
        lax.fori_loop(0, n_heads, body, 0)

    @pl.when(kj == qi)
    def _diag():
        row = lax.broadcasted_iota(jnp.int32, (tq, tk), 0)
        col = lax.broadcasted_iota(jnp.int32, (tq, tk), 1)
        mask = col <= row
        lam = lam_ref[...]
        g = g_ref[...] * (1.0 - lam_init)

        def body(h, c):
            _flash_update(2 * h, qlo_ref[h], k_ref[h], v_ref[h], mask, m_ref, l_ref, acc_ref)
            _flash_update(2 * h + 1, qhi_ref[h], k_ref[h], v_ref[h], mask, m_ref, l_ref, acc_ref)
            o = acc_ref[2 * h] / l_ref[2 * h] - lam * (acc_ref[2 * h + 1] / l_ref[2 * h + 1])
            o = o * lax.rsqrt(jnp.mean(o * o, axis=1, keepdims=True) + EPS) * g
            o_ref[h] = o.astype(BF16)
            return c
        lax.fori_loop(0, n_heads, body, 0)


def _prompt_attention(qlo, qhi, kb, vb, kmeta, vmeta, lam_row, g_row, batch, lam_init, tile):
    n_heads, m, _ = qlo.shape
    seq = m // batch
    nq = seq // tile
    qi_list, kj_list = [], []
    for qi in range(nq):
        for kj in range(-1, qi + 1):
            qi_list.append(qi)
            kj_list.append(kj)
    qi_tab = jnp.asarray(qi_list, jnp.int32)
    kj_tab = jnp.asarray(kj_list, jnp.int32)
    n_steps = len(qi_list)
    tmeta = kmeta.shape[1]
    qspec = pl.BlockSpec((n_heads, tile, LANE), lambda b, s, qt, kt: (0, b * nq + qt[s], 0))
    kspec = pl.BlockSpec((n_heads, tile, LANE), lambda b, s, qt, kt: (0, b * nq + jnp.maximum(kt[s], 0), 0))
    mspec = pl.BlockSpec((n_heads, tmeta, LANE), lambda b, s, qt, kt: (0, 0, 0))
    rspec = pl.BlockSpec((1, LANE), lambda b, s, qt, kt: (0, 0))
    grid_spec = pltpu.PrefetchScalarGridSpec(
        num_scalar_prefetch=2,
        grid=(batch, n_steps),
        in_specs=[qspec, qspec, kspec, kspec, mspec, mspec, rspec, rspec],
        out_specs=qspec,
        scratch_shapes=[pltpu.VMEM((2 * n_heads, tile, LANE), F32),
                        pltpu.VMEM((2 * n_heads, tile, LANE), F32),
                        pltpu.VMEM((2 * n_heads, tile, LANE), F32)],
    )
    return pl.pallas_call(
        functools.partial(_attn_kernel, n_heads=n_heads, lam_init=lam_init),
        out_shape=jax.ShapeDtypeStruct((n_heads, m, LANE), BF16),
        grid_spec=grid_spec,
        compiler_params=_cparams("parallel", "arbitrary"),
        name="prompt_attention",
    )(qi_tab, kj_tab, qlo, qhi, kb, vb, kmeta, vmeta, lam_row, g_row)


def _paged_kernel(pt_ref, q_ref, *refs, n_heads, n_tok, pages, page, lam_init):
    k_refs = refs[:pages]
    v_refs = refs[pages:2 * pages]
    kn_ref, vn_ref, lam_ref, g_ref, o_ref, qbd_ref, kb_ref, vb_ref, m_ref, l_ref, acc_ref = refs[2 * pages:]
    j = pl.program_id(1)
    rows = 2 * n_heads * n_tok
    d = q_ref.shape[1]

    @pl.when(j == 0)
    def _init():
        q = q_ref[...]
        qt = jnp.tile(q, (rows // n_tok, 1))
        r = lax.broadcasted_iota(jnp.int32, (rows, d), 0)
        c = lax.broadcasted_iota(jnp.int32, (rows, d), 1)
        qbd_ref[...] = jnp.where((c // D_QK) == (r // n_tok), qt, 0.0).astype(BF16)
        m_ref[...] = jnp.full(m_ref.shape, NEG, F32)
        l_ref[...] = jnp.zeros(l_ref.shape, F32)
        acc_ref[...] = jnp.zeros(acc_ref.shape, F32)

    def update(k, v, mask):
        s = _dot_nt(qbd_ref[...], k)
        if mask is not None:
            s = jnp.where(mask, s, NEG)
        m_prev = m_ref[...]
        m_new = jnp.maximum(m_prev, jnp.max(s, axis=1, keepdims=True))
        alpha = jnp.exp(m_prev - m_new)
        p = jnp.exp(s - jnp.tile(m_new, (1, s.shape[1] // LANE)))
        l_ref[...] = alpha * l_ref[...] + jnp.sum(p, axis=1, keepdims=True)
        acc_ref[...] = jnp.tile(alpha, (1, d // LANE)) * acc_ref[...] + _dot(p.astype(BF16), v)
        m_ref[...] = m_new

    for p in range(pages):
        kb_ref[p * page:(p + 1) * page, :] = k_refs[p][...].astype(BF16)
        vb_ref[p * page:(p + 1) * page, :] = v_refs[p][...].astype(BF16)
    update(kb_ref[...], vb_ref[...], None)

    @pl.when(j == pl.num_programs(1) - 1)
    def _final():
        r = lax.broadcasted_iota(jnp.int32, (rows, kn_ref.shape[0]), 0)
        c = lax.broadcasted_iota(jnp.int32, (rows, kn_ref.shape[0]), 1)
        update(kn_ref[...], vn_ref[...], c <= (r % n_tok))
        o = acc_ref[...] / jnp.tile(l_ref[...], (1, d // LANE))
        lam = lam_ref[...]
        g = g_ref[...] * (1.0 - lam_init)
        for h in range(n_heads):
            r0 = 2 * h * n_tok
            o1 = o[r0:r0 + n_tok, h * LANE:(h + 1) * LANE]
            o2 = o[r0 + n_tok:r0 + 2 * n_tok, h * LANE:(h + 1) * LANE]
            oh = o1 - lam * o2
            oh = oh * lax.rsqrt(jnp.mean(oh * oh, axis=1, keepdims=True) + EPS) * g
            o_ref[:, h * LANE:(h + 1) * LANE] = oh


def _paged_attention(q, knew, vnew, cache_k, cache_v, page_table, lam_row, g_row, n_heads, lam_init, pages):
    bd, n_tok, d = q.shape
    n_pool, page = cache_k.shape[0], cache_k.shape[1]
    n_pages = page_table.shape[1]
    rows = 2 * n_heads * n_tok

    def page_spec(p):
        return pl.BlockSpec((None, page, d), lambda b, j, pt: (pt[b * n_pages + j * pages + p], 0, 0))

    bspec = pl.BlockSpec((None, n_tok, d), lambda b, j, pt: (b, 0, 0))
    nspec = pl.BlockSpec((None, knew.shape[1], d), lambda b, j, pt: (b, 0, 0))
    rspec = pl.BlockSpec((1, LANE), lambda b, j, pt: (0, 0))
    grid_spec = pltpu.PrefetchScalarGridSpec(
        num_scalar_prefetch=1,
        grid=(bd, n_pages // pages),
        in_specs=[bspec] + [page_spec(p) for p in range(pages)] * 2 + [nspec, nspec, rspec, rspec],
        out_specs=bspec,
        scratch_shapes=[pltpu.VMEM((rows, d), BF16),
                        pltpu.VMEM((pages * page, d), BF16),
                        pltpu.VMEM((pages * page, d), BF16),
                        pltpu.VMEM((rows, LANE), F32),
                        pltpu.VMEM((rows, LANE), F32),
                        pltpu.VMEM((rows, d), F32)],
    )
    args = [page_table.reshape(-1), q] + [cache_k] * pages + [cache_v] * pages + [knew, vnew, lam_row, g_row]
    return pl.pallas_call(
        functools.partial(_paged_kernel, n_heads=n_heads, n_tok=n_tok, pages=pages, page=page,
                          lam_init=lam_init),
        out_shape=jax.ShapeDtypeStruct((bd, n_tok, d), F32),
        grid_spec=grid_spec,
        compiler_params=_cparams("parallel", "arbitrary"),
        name="paged_attention",
    )(*args)


def _mlstm_kernel(qk_ref, v_ref, om_ref, gcol_ref, grow_ref, cinit_ref, c0_ref, n0_ref, m0_ref,
                  cw_ref, cb_ref, gn_ref, y_ref, c_ref, n_ref, m_ref, xpad_ref, *, n_heads):
    ci = pl.program_id(1)
    t = qk_ref.shape[0]
    dm = v_ref.shape[1]
    dh = dm // n_heads
    hist = CONV_W - 1

    @pl.when(ci == 0)
    def _init():
        xpad_ref[0:8, :] = cinit_ref[...]
        c_ref[...] = c0_ref[...]
        n_ref[...] = n0_ref[...]
        m_ref[...] = m0_ref[...]

    x = qk_ref[...]
    xpad_ref[8:8 + t, :] = x
    y = cb_ref[...] + cw_ref[hist:hist + 1, :] * x
    for j in range(hist):
        y = y + cw_ref[j:j + 1, :] * xpad_ref[8 - hist + j:8 - hist + j + t, :]
    xpad_ref[0:8, :] = x[t - 8:t, :]
    y = y * _sigmoid(y)
    q_all = y[:, :dm].astype(BF16)
    k_all = (y[:, dm:] * (dh ** -0.5)).astype(BF16)

    gcol = gcol_ref[...]
    grow = grow_ref[...]
    r = lax.broadcasted_iota(jnp.int32, (t, t), 0)
    c = lax.broadcasted_iota(jnp.int32, (t, t), 1)
    causal = c <= r
    tril = jnp.where(causal, 1.0, 0.0).astype(F32)
    triu = jnp.where(r <= c, 1.0, 0.0).astype(F32)
    bc_all = jnp.dot(tril, gcol, preferred_element_type=F32, precision=lax.Precision.HIGHEST)
    br_all = jnp.dot(grow, triu, preferred_element_type=F32, precision=lax.Precision.HIGHEST)

    for h in range(n_heads):
        sl = slice(h * dh, (h + 1) * dh)
        qh, kh, vh = q_all[:, sl], k_all[:, sl], v_ref[:, sl]
        m_prev = m_ref[h][:, 0:1]
        b_c = bc_all[:, n_heads + h:n_heads + h + 1]
        li_c = gcol[:, h:h + 1]
        b_r = br_all[n_heads + h:n_heads + h + 1, :]
        li_r = grow[h:h + 1, :]
        log_d = jnp.where(causal, b_c - b_r + li_r, NEG)
        m_t = jnp.maximum(b_c + m_prev, jnp.max(log_d, axis=1, keepdims=True))
        s = _dot_nt(qh, kh) * jnp.exp(log_d - m_t)
        w_inter = jnp.exp(b_c + m_prev - m_t)
        ch = c_ref[h]
        num = _dot(s.astype(BF16), vh) + w_inter * _dot_nt(qh, ch.astype(BF16))
        nh = n_ref[h]
        den = jnp.sum(s, axis=1, keepdims=True) + w_inter * jnp.sum(qh.astype(F32) * nh, axis=1, keepdims=True)
        hout = num / jnp.maximum(jnp.abs(den), jnp.exp(-m_t))
        hn = hout * lax.rsqrt(jnp.mean(hout * hout, axis=1, keepdims=True) + EPS) * gn_ref[:, sl]
        y_ref[:, sl] = (_sigmoid(om_ref[:, sl].astype(F32)) * hn).astype(BF16)
        b_last = b_r[:, t - 1:t]
        log_w_r = b_last - b_r + li_r
        m_new = jnp.maximum(b_last + m_prev, jnp.max(log_w_r, axis=1, keepdims=True))
        w_c = jnp.exp(b_last - b_c + li_c - m_new)
        decay = jnp.exp(b_last + m_prev - m_new)
        wv = (w_c * vh.astype(F32)).astype(BF16)
        c_ref[h] = decay * ch + _dot_tn(wv, kh)
        n_ref[h] = decay * nh + jnp.sum(w_c * kh.astype(F32), axis=0, keepdims=True)
        m_ref[h] = jnp.broadcast_to(m_new, (1, LANE))


def _mlstm(qk, v, om, gcol, grow, conv_init, c0, n0, m0, conv_w, conv_b, g_norm, chunk, shared_init):
    bs, seq, dm2 = qk.shape
    dm = dm2 // 2
    n_heads, dh = c0.shape[1], c0.shape[2]
    ng = gcol.shape[2]
    nc = seq // chunk
    sidx = (lambda b: 0) if shared_init else (lambda b: b)
    row = lambda w: pl.BlockSpec((None, chunk, w), lambda b, i: (b, i, 0))
    return pl.pallas_call(
        functools.partial(_mlstm_kernel, n_heads=n_heads),
        out_shape=(jax.ShapeDtypeStruct((bs, seq, dm), BF16),
                   jax.ShapeDtypeStruct((bs, n_heads, dh, dh), F32),
                   jax.ShapeDtypeStruct((bs, n_heads, 1, dh), F32),
                   jax.ShapeDtypeStruct((bs, n_heads, 1, LANE), F32)),
        grid=(bs, nc),
        in_specs=[row(dm2), row(dm), row(dm), row(ng),
                  pl.BlockSpec((None, ng, chunk), lambda b, i: (b, 0, i)),
                  pl.BlockSpec((None, 8, dm2), lambda b, i: (sidx(b), 0, 0)),
                  pl.BlockSpec((None, n_heads, dh, dh), lambda b, i: (sidx(b), 0, 0, 0)),
                  pl.BlockSpec((None, n_heads, 1, dh), lambda b, i: (sidx(b), 0, 0, 0)),
                  pl.BlockSpec((None, n_heads, 1, LANE), lambda b, i: (sidx(b), 0, 0, 0)),
                  pl.BlockSpec((CONV_W, dm2), lambda b, i: (0, 0)),
                  pl.BlockSpec((1, dm2), lambda b, i: (0, 0)),
                  pl.BlockSpec((1, dm), lambda b, i: (0, 0))],
        out_specs=(row(dm),
                   pl.BlockSpec((None, n_heads, dh, dh), lambda b, i: (b, 0, 0, 0)),
                   pl.BlockSpec((None, n_heads, 1, dh), lambda b, i: (b, 0, 0, 0)),
                   pl.BlockSpec((None, n_heads, 1, LANE), lambda b, i: (b, 0, 0, 0))),
        scratch_shapes=[pltpu.VMEM((chunk + 8, dm2), F32)],
        compiler_params=_cparams("parallel", "arbitrary"),
        name="mlstm",
    )(qk, v, om, gcol, grow, conv_init, c0, n0, m0, conv_w, conv_b, g_norm)


def _layer_norm(z, g, b):
    mu = jnp.mean(z, axis=1, keepdims=True)
    zc = z - mu
    var = jnp.mean(zc * zc, axis=1, keepdims=True)
    return zc * lax.rsqrt(var + EPS) * g + b


def _merge_kernel(x_ref, ya_ref, ym_ref, ga_ref, gm_ref, wba_ref, wbm_ref, wout_ref, g_ref, b_ref,
                  wrh_ref, wrl_ref, h_ref, hb_ref, s_ref, *, head_major):
    if head_major:
        ya = jnp.concatenate([ya_ref[h] for h in range(ya_ref.shape[0])], axis=1)
    else:
        ya = ya_ref[...]
    a = _dot(ya, wba_ref[...])
    m = _dot(ym_ref[...], wbm_ref[...])
    mix = ga_ref[...].astype(F32) * a + gm_ref[...].astype(F32) * m
    z = ALPHA * x_ref[...] + _dot(mix.astype(BF16), wout_ref[...])
    h = _layer_norm(z, g_ref[...], b_ref[...])
    h_ref[...] = h
    hh = h.astype(BF16)
    hb_ref[...] = hh
    hl = (h - hh.astype(F32)).astype(BF16)
    wrh = wrh_ref[...]
    s_ref[...] = _sigmoid(_dot(hh, wrh) + _dot(hl, wrh) + _dot(hh, wrl_ref[...]))


def _merge(x, ya, ym, ga, gm, wba, wbm, wout, g, b, wrh, wrl, tm, head_major):
    m, d = x.shape
    ne = wrh.shape[1]
    row = pl.BlockSpec((tm, d), lambda i: (i, 0))
    full = lambda a: pl.BlockSpec(a.shape, lambda i: (0,) * a.ndim)
    ya_spec = pl.BlockSpec((ya.shape[0], tm, LANE), lambda i: (0, i, 0)) if head_major else row
    return pl.pallas_call(
        functools.partial(_merge_kernel, head_major=head_major),
        out_shape=(jax.ShapeDtypeStruct((m, d), F32), jax.ShapeDtypeStruct((m, d), BF16),
                   jax.ShapeDtypeStruct((m, ne), F32)),
        grid=(m // tm,),
        in_specs=[row, ya_spec, row, row, row, full(wba), full(wbm), full(wout), full(g), full(b),
                  full(wrh), full(wrl)],
        out_specs=(row, row, pl.BlockSpec((tm, ne), lambda i: (i, 0))),
        compiler_params=_cparams("parallel"),
        name="merge_ln_router",
    )(x, ya, ym, ga, gm, wba, wbm, wout, g, b, wrh, wrl)


def _expert_kernel(be_ref, nu_ref, x_ref, w1_ref, w3_ref, w2_ref, y_ref, w1b_ref, w3b_ref, w2b_ref):
    i = pl.program_id(0)
    e = be_ref[i]
    e_prev = be_ref[jnp.maximum(i - 1, 0)]

    @pl.when(i < nu_ref[0])
    def _run():
        @pl.when(jnp.logical_or(i == 0, e != e_prev))
        def _cast():
            w1b_ref[...] = w1_ref[...].astype(BF16)
            w3b_ref[...] = w3_ref[...].astype(BF16)
            w2b_ref[...] = w2_ref[...].astype(BF16)

        x = x_ref[...]
        a = _dot(x, w1b_ref[...])
        g = (a * _sigmoid(a)) * _dot(x, w3b_ref[...])
        y_ref[...] = _dot(g.astype(BF16), w2b_ref[...]).astype(BF16)


def _experts(xs, blk_e, n_used, w1, w3, w2, blk):
    n_rows, d = xs.shape
    n_blk = n_rows // blk
    de = w1.shape[2]
    grid_spec = pltpu.PrefetchScalarGridSpec(
        num_scalar_prefetch=2,
        grid=(n_blk,),
        in_specs=[pl.BlockSpec((blk, d), lambda i, be, nu: (i, 0)),
                  pl.BlockSpec((None, d, de), lambda i, be, nu: (be[i], 0, 0)),
                  pl.BlockSpec((None, d, de), lambda i, be, nu: (be[i], 0, 0)),
                  pl.BlockSpec((None, de, d), lambda i, be, nu: (be[i], 0, 0))],
        out_specs=pl.BlockSpec((blk, d), lambda i, be, nu: (i, 0)),
        scratch_shapes=[pltpu.VMEM((d, de), BF16), pltpu.VMEM((d, de), BF16), pltpu.VMEM((de, d), BF16)],
    )
    return pl.pallas_call(
        _expert_kernel,
        out_shape=jax.ShapeDtypeStruct((n_rows, d), BF16),
        grid_spec=grid_spec,
        compiler_params=_cparams("arbitrary"),
        name="experts",
    )(blk_e, n_used, xs, w1, w3, w2)


def _final_kernel(h_ref, hb_ref, moe_ref, ws1_ref, ws3_ref, ws2_ref, g_ref, b_ref, o_ref):
    x = hb_ref[...]
    a = _dot(x, ws1_ref[...])
    gg = (a * _sigmoid(a)) * _dot(x, ws3_ref[...])
    f = moe_ref[...] + _dot(gg.astype(BF16), ws2_ref[...])
    o_ref[...] = _layer_norm(ALPHA * h_ref[...] + f, g_ref[...], b_ref[...])


def _final(h, hb, moe, ws1, ws3, ws2, g, b, tm):
    m, d = h.shape
    row = pl.BlockSpec((tm, d), lambda i: (i, 0))
    full = lambda a: pl.BlockSpec(a.shape, lambda i: (0,) * a.ndim)
    return pl.pallas_call(
        _final_kernel,
        out_shape=jax.ShapeDtypeStruct((m, d), F32),
        grid=(m // tm,),
        in_specs=[row, row, row, full(ws1), full(ws3), full(ws2), full(g), full(b)],
        out_specs=row,
        compiler_params=_cparams("parallel"),
        name="shared_ln",
    )(h, hb, moe, ws1, ws3, ws2, g, b)


def _rope_tables(pos):
    half = ROPE_DIM // 2
    inv = ROPE_THETA ** (-jnp.arange(half, dtype=F32) * 2.0 / ROPE_DIM)
    ang = pos.astype(F32)[:, None] * inv
    cos8, sin8 = jnp.cos(ang), jnp.sin(ang)
    n = pos.shape[0]
    cos = jnp.concatenate([cos8, cos8, jnp.ones((n, D_QK - ROPE_DIM), F32)], axis=1)
    sin = jnp.concatenate([-sin8, sin8, jnp.zeros((n, D_QK - ROPE_DIM), F32)], axis=1)
    return jnp.tile(cos, (1, LANE // D_QK)), jnp.tile(sin, (1, LANE // D_QK))


def _route(s, b_router):
    t, n_exp = s.shape
    s_sel = s + b_router.astype(F32)
    grp = lax.top_k(s_sel.reshape(t, N_GROUPS, n_exp // N_GROUPS), 2)[0].sum(-1)
    _, gidx = lax.top_k(grp, TOPK_GROUPS)
    gmask = jnp.any(gidx[..., None] == jnp.arange(N_GROUPS), axis=1)
    emask = jnp.repeat(gmask, n_exp // N_GROUPS, axis=1)
    _, idx = lax.top_k(jnp.where(emask, s_sel, -jnp.inf), TOP_K)
    g = jnp.take_along_axis(s, idx, axis=1)
    g = g / jnp.sum(g, -1, keepdims=True) * ROUTE_SCALE
    return idx, g


def _dispatch_plan(idx, n_exp, blk):
    t = idx.shape[0]
    tk = t * TOP_K
    n_blk = -(-tk // blk) + n_exp
    flat_e = idx.reshape(tk)
    order = jnp.argsort(flat_e)
    e_sorted = flat_e[order]
    tok_sorted = (order // TOP_K).astype(jnp.int32)
    counts = jnp.bincount(flat_e, length=n_exp)
    padded = (counts + blk - 1) // blk * blk
    pad_end = jnp.cumsum(padded)
    start = jnp.cumsum(counts) - counts
    dest_sorted = (pad_end[e_sorted] - padded[e_sorted] + jnp.arange(tk) - start[e_sorted]).astype(jnp.int32)
    rows = jnp.zeros((n_blk * blk,), jnp.int32).at[dest_sorted].set(tok_sorted)
    dest = jnp.zeros((tk,), jnp.int32).at[order].set(dest_sorted).reshape(t, TOP_K)
    blk_e = jnp.minimum(jnp.searchsorted(pad_end, jnp.arange(n_blk) * blk, side='right'), n_exp - 1).astype(jnp.int32)
    n_used = (pad_end[-1] // blk).astype(jnp.int32).reshape(1)
    return rows, dest, blk_e, n_used


def _row_tile(m, pref):
    t = pref
    while m % t:
        t //= 2
    return t


def kernel(x_prompt, x_sample, cache_k, cache_v, page_table, state_C, state_n, state_m, state_conv, meta_tokens, w_in, b_in, lam_q1, lam_k1, lam_q2, lam_k2, g_subln, conv_w, conv_b, g_mnorm, w_ba, w_bm, w_out, ln1_g, ln1_b, w_router, b_router, w1, w3, w2, ws1, ws3, ws2, ln2_g, ln2_b):
    B, SEQ, D = x_prompt.shape
    BD, S, _ = x_sample.shape
    NHA, DVA = cache_k.shape[3], cache_v.shape[4]
    NHM, DHM = state_C.shape[2], state_C.shape[3]
    n_exp = w_router.shape[2]
    n_pages = page_table.shape[1]
    page = cache_k.shape[2]
    l = 0
    lam_init = 0.8 - 0.6 * math.exp(-0.3 * l)
    lam = (jnp.exp(jnp.sum(lam_q1[l] * lam_k1[l])) - jnp.exp(jnp.sum(lam_q2[l] * lam_k2[l])) + lam_init)
    lam_row = jnp.broadcast_to(lam.astype(F32), (1, LANE))
    g_row = g_subln[l].astype(F32).reshape(1, LANE)

    w = w_in[l]
    bias = b_in[l].astype(F32)
    o_mix = 3 * D
    o_if = o_mix + 4 * D
    o_g = o_if + 2 * NHM
    w_qkv = w[:, :o_mix].astype(BF16)
    w_mix = w[:, o_mix:o_if].astype(BF16)
    w_g = w[:, o_g:].astype(BF16)
    w_if = w[:, o_if:o_g]
    b_qkv = bias[None, :o_mix]
    b_mix = bias[None, o_mix:o_if]
    b_g = bias[None, o_g:]
    b_if = bias[o_if:o_g]

    MP = B * SEQ
    xp = x_prompt.reshape(MP, D)
    xs = x_sample.reshape(BD * S, D)
    xm = jnp.zeros((PAD_CHUNK, D), F32).at[:N_META].set(meta_tokens.astype(F32))
    tmp = _row_tile(SEQ, 512)
    tms = _row_tile(BD * S, 256)

    cos_p, sin_p = _rope_tables(N_META + jnp.arange(SEQ))
    cos_s, sin_s = _rope_tables(jnp.tile(n_pages * page + jnp.arange(S), BD))
    cos_m, sin_m = _rope_tables(jnp.arange(PAD_CHUNK))

    xp_bf, xs_bf, xm_bf = xp.astype(BF16), xs.astype(BF16), xm.astype(BF16)
    qlo_p, qhi_p, kf_p, kb_p, vf_p, vb_p = _proj_qkv(xp_bf, w_qkv, b_qkv, cos_p, sin_p, tmp, NHA)
    qlo_s, qhi_s, kf_s, _, vf_s, _ = _proj_qkv(xs_bf, w_qkv, b_qkv, cos_s, sin_s, tms, NHA)
    _, _, kf_m, kb_m, vf_m, vb_m = _proj_qkv(xm_bf, w_qkv, b_qkv, cos_m, sin_m, PAD_CHUNK, NHA)

    qk_p, vm_p, om_p = _proj_mix(xp_bf, w_mix, b_mix, tmp)
    qk_s, vm_s, om_s = _proj_mix(xs_bf, w_mix, b_mix, tms)
    qk_m, vm_m, om_m = _proj_mix(xm_bf, w_mix, b_mix, PAD_CHUNK)

    ga_p, gm_p = _proj_gates(xp_bf, w_g, b_g, tmp)
    ga_s, gm_s = _proj_gates(xs_bf, w_g, b_g, tms)

    if_p = _proj_if(xp, w_if, b_if, tmp, NHM)[:, :2 * NHM]
    if_s = _proj_if(xs, w_if, b_if, tms, NHM)[:, :2 * NHM]
    if_m = _proj_if(xm, w_if, b_if, PAD_CHUNK, NHM)[:, :2 * NHM]

    ya_p = _prompt_attention(qlo_p, qhi_p, kb_p, vb_p, kb_m, vb_m, lam_row, g_row, B, lam_init,
                             _row_tile(SEQ, ATTN_TILE))
    q_s = jnp.transpose(qlo_s + qhi_s, (1, 0, 2)).reshape(BD, S, D).astype(F32)
    kn = jnp.zeros((BD, LANE, D), BF16).at[:, :S].set(kf_s.reshape(BD, S, D).astype(BF16))
    vn = jnp.zeros((BD, LANE, D), BF16).at[:, :S].set(vf_s.reshape(BD, S, D).astype(BF16))
    ya_s = _paged_attention(q_s, kn, vn, cache_k[l].reshape(-1, page, D), cache_v[l].reshape(-1, page, D),
                            page_table, lam_row, g_row, NHA, lam_init, math.gcd(PAGES_PER_STEP, n_pages))

    cw = conv_w[l].astype(F32)
    cb = conv_b[l].astype(F32)[None]
    gn = g_mnorm[l].astype(F32)[None]
    pad_gate = jnp.concatenate([jnp.full((NHM,), NEG, F32), jnp.zeros((NHM,), F32)])

    def gate_layouts(g, bs, real, total):
        g = g.reshape(bs, -1, 2 * NHM)[:, :real]
        if total > real:
            g = jnp.concatenate([g, jnp.broadcast_to(pad_gate, (bs, total - real, 2 * NHM))], axis=1)
        return g, jnp.transpose(g, (0, 2, 1))

    def pad_rows(a, bs, real, total):
        a = a.reshape(bs, -1, a.shape[-1])[:, :real]
        return jnp.pad(a, ((0, 0), (0, total - real), (0, 0)))

    gcol_m, grow_m = gate_layouts(if_m, 1, N_META, PAD_CHUNK)
    zC = jnp.zeros((1, NHM, DHM, DHM), F32)
    zn = jnp.zeros((1, NHM, 1, DHM), F32)
    zm = jnp.zeros((1, NHM, 1, LANE), F32)
    _, c_m, n_m, m_m = _mlstm(pad_rows(qk_m, 1, N_META, PAD_CHUNK), pad_rows(vm_m, 1, N_META, PAD_CHUNK),
                              pad_rows(om_m, 1, N_META, PAD_CHUNK), gcol_m, grow_m,
                              jnp.zeros((1, 8, 2 * D), F32), zC, zn, zm, cw, cb, gn, PAD_CHUNK, True)
    hist_m = jnp.zeros((1, 8, 2 * D), F32).at[0, 8 - (CONV_W - 1):].set(qk_m[N_META - (CONV_W - 1):N_META])
    gcol_p, grow_p = gate_layouts(if_p, B, SEQ, SEQ)
    ym_p, c_p, n_p, m_p = _mlstm(qk_p.reshape(B, SEQ, 2 * D), vm_p.reshape(B, SEQ, D), om_p.reshape(B, SEQ, D),
                                 gcol_p, grow_p, hist_m, c_m, n_m, m_m, cw, cb, gn,
                                 _row_tile(SEQ, MLSTM_CHUNK), True)
    gcol_s, grow_s = gate_layouts(if_s, BD, S, PAD_CHUNK)
    hist_s = jnp.zeros((BD, 8, 2 * D), F32).at[:, 8 - (CONV_W - 1):].set(state_conv[l].astype(F32))
    ym_s, c_s, n_s, m_s = _mlstm(pad_rows(qk_s, BD, S, PAD_CHUNK), pad_rows(vm_s, BD, S, PAD_CHUNK),
                                 pad_rows(om_s, BD, S, PAD_CHUNK), gcol_s, grow_s, hist_s,
                                 state_C[l].astype(F32), state_n[l].astype(F32)[:, :, None, :],
                                 jnp.broadcast_to(state_m[l].astype(F32)[:, :, None, None], (BD, NHM, 1, LANE)),
                                 cw, cb, gn, PAD_CHUNK, False)
    ym_s = ym_s[:, :S].reshape(BD * S, D)

    wba, wbm, wout = w_ba[l].astype(BF16), w_bm[l].astype(BF16), w_out[l].astype(BF16)
    g1, b1 = ln1_g[l].astype(F32)[None], ln1_b[l].astype(F32)[None]
    wr = w_router[l].astype(F32)
    wrh = wr.astype(BF16)
    wrl = (wr - wrh.astype(F32)).astype(BF16)
    h_p, hb_p, s_p = _merge(xp, ya_p, ym_p.reshape(MP, D), ga_p, gm_p, wba, wbm, wout, g1, b1, wrh, wrl,
                            _row_tile(MP, 256), True)
    h_s, hb_s, s_s = _merge(xs, ya_s.reshape(BD * S, D).astype(BF16), ym_s, ga_s, gm_s, wba, wbm, wout,
                            g1, b1, wrh, wrl, tms, False)

    hb_all = jnp.concatenate([hb_p, hb_s], axis=0)
    s_all = jnp.concatenate([s_p, s_s], axis=0)
    idx, gate = _route(s_all, b_router[l])
    rows, dest, blk_e, n_used = _dispatch_plan(idx, n_exp, MOE_BLOCK)
    y_sorted = _experts(hb_all[rows], blk_e, n_used, w1[l], w3[l], w2[l], MOE_BLOCK)
    moe = jnp.sum(y_sorted[dest].astype(F32) * gate[..., None], axis=1)

    ws1b, ws3b, ws2b = ws1[l].astype(BF16), ws3[l].astype(BF16), ws2[l].astype(BF16)
    g2, b2 = ln2_g[l].astype(F32)[None], ln2_b[l].astype(F32)[None]
    y_p = _final(h_p, hb_p, moe[:MP], ws1b, ws3b, ws2b, g2, b2, _row_tile(MP, 512))
    y_s = _final(h_s, hb_s, moe[MP:], ws1b, ws3b, ws2b, g2, b2, tms)

    def with_meta(meta_rows, real):
        meta = jnp.broadcast_to(meta_rows[None, :N_META], (B, N_META, D))
        return jnp.concatenate([meta, real.reshape(B, SEQ, D)], axis=1).reshape(1, B, N_META + SEQ, NHA, DVA)

    k_prompt = with_meta(kf_m, kf_p)
    v_prompt = with_meta(vf_m, vf_p)
    conv_prompt = qk_p.reshape(B, SEQ, 2 * D)[:, SEQ - (CONV_W - 1):][None]
    conv_sample = qk_s.reshape(BD, S, 2 * D)[:, S - (CONV_W - 1):][None]
    return (y_p.reshape(B, SEQ, D), y_s.reshape(BD, S, D),
            k_prompt, v_prompt,
            c_p[None], n_p[:, :, 0][None], m_p[:, :, 0, 0][None], conv_prompt,
            kf_s.reshape(1, BD, S, NHA, DVA), vf_s.reshape(1, BD, S, NHA, DVA),
            c_s[None], n_s[:, :, 0][None], m_s[:, :, 0, 0][None], conv_sample)
```

```python
import functools
import math

import jax
import jax.numpy as jnp
from jax import lax
from jax.experimental import pallas as pl
from jax.experimental.pallas import tpu as pltpu

F32 = jnp.float32
BF16 = jnp.bfloat16

DEPTH = 1
N_META = 16
D_QK = 64
ROPE_DIM = D_QK // 4
ROPE_THETA = 500000.0
CONV_W = 4
TOP_K = 8
N_GROUPS = 8
TOPK_GROUPS = 4
ROUTE_SCALE = 2.5
ALPHA = (2 * DEPTH) ** 0.25
EPS = 1e-5
NEG = -1e30

LANE = 128
VMEM_LIMIT = 52 * 1024 * 1024

ATTN_TILE = 512
MLSTM_CHUNK = 256
PAD_CHUNK = 128
PAGES_PER_STEP = 8
MOE_BLOCK = 256


def _cparams(*sem):
    return pltpu.CompilerParams(dimension_semantics=sem, vmem_limit_bytes=VMEM_LIMIT)


def _dot(a, b):
    return jnp.dot(a, b, preferred_element_type=F32)


def _dot_nt(a, b):
    return lax.dot_general(a, b, (((1,), (1,)), ((), ())), preferred_element_type=F32)


def _dot_tn(a, b):
    return lax.dot_general(a, b, (((0,), (0,)), ((), ())), preferred_element_type=F32)


def _split3(a):
    a1 = a.astype(BF16)
    r1 = a - a1.astype(F32)
    a2 = r1.astype(BF16)
    a3 = (r1 - a2.astype(F32)).astype(BF16)
    return a1, a2, a3


def _sigmoid(x):
    return 1.0 / (1.0 + jnp.exp(-x))


def _qkv_kernel(x_ref, w_ref, b_ref, cos_ref, sin_ref,
                qlo_ref, qhi_ref, kf_ref, kb_ref, vf_ref, vb_ref, *, n_heads, d_model):
    x = x_ref[...]
    cos = cos_ref[...]
    sin = sin_ref[...]
    lane = lax.broadcasted_iota(jnp.int32, cos.shape, 1)
    first = (lane % D_QK) < (ROPE_DIM // 2)
    lo_half = lane < D_QK
    half = ROPE_DIM // 2

    def rope(s):
        up = pltpu.roll(s, LANE - half, axis=1)
        dn = pltpu.roll(s, half, axis=1)
        return s * cos + jnp.where(first, up, dn) * sin

    zq = _dot(x, w_ref[:, 0:d_model]) + b_ref[:, 0:d_model]
    for h in range(n_heads):
        q = rope(zq[:, h * LANE:(h + 1) * LANE]) * (D_QK ** -0.5)
        qlo_ref[h] = jnp.where(lo_half, q, 0.0).astype(BF16)
        qhi_ref[h] = jnp.where(lo_half, 0.0, q).astype(BF16)
    zk = _dot(x, w_ref[:, d_model:2 * d_model]) + b_ref[:, d_model:2 * d_model]
    for h in range(n_heads):
        k = rope(zk[:, h * LANE:(h + 1) * LANE])
        kf_ref[:, h * LANE:(h + 1) * LANE] = k
        kb_ref[h] = k.astype(BF16)
    zv = _dot(x, w_ref[:, 2 * d_model:3 * d_model]) + b_ref[:, 2 * d_model:3 * d_model]
    vf_ref[...] = zv
    for h in range(n_heads):
        vb_ref[h] = zv[:, h * LANE:(h + 1) * LANE].astype(BF16)


def _proj_qkv(x_bf, w_bf, bias, cos, sin, tm, n_heads):
    m, d = x_bf.shape
    n_pos = cos.shape[0] // tm
    hm = jax.ShapeDtypeStruct((n_heads, m, LANE), BF16)
    fm = jax.ShapeDtypeStruct((m, d), F32)
    hspec = pl.BlockSpec((n_heads, tm, LANE), lambda i: (0, i, 0))
    fspec = pl.BlockSpec((tm, d), lambda i: (i, 0))
    pspec = pl.BlockSpec((tm, LANE), lambda i: (i % n_pos, 0))
    return pl.pallas_call(
        functools.partial(_qkv_kernel, n_heads=n_heads, d_model=d),
        out_shape=(hm, hm, fm, hm, fm, hm),
        grid=(m // tm,),
        in_specs=[fspec, pl.BlockSpec((d, 3 * d), lambda i: (0, 0)),
                  pl.BlockSpec((1, 3 * d), lambda i: (0, 0)), pspec, pspec],
        out_specs=(hspec, hspec, fspec, hspec, fspec, hspec),
        compiler_params=_cparams("parallel"),
        name="proj_qkv",
    )(x_bf, w_bf, bias, cos, sin)


def _mix_proj_kernel(x_ref, w_ref, b_ref, qk_ref, v_ref, om_ref, *, d_model):
    x = x_ref[...]
    d = d_model
    qk_ref[...] = _dot(x, w_ref[:, 0:2 * d]) + b_ref[:, 0:2 * d]
    v_ref[...] = (_dot(x, w_ref[:, 2 * d:3 * d]) + b_ref[:, 2 * d:3 * d]).astype(BF16)
    om_ref[...] = (_dot(x, w_ref[:, 3 * d:4 * d]) + b_ref[:, 3 * d:4 * d]).astype(BF16)


def _proj_mix(x_bf, w_bf, bias, tm):
    m, d = x_bf.shape
    return pl.pallas_call(
        functools.partial(_mix_proj_kernel, d_model=d),
        out_shape=(jax.ShapeDtypeStruct((m, 2 * d), F32), jax.ShapeDtypeStruct((m, d), BF16),
                   jax.ShapeDtypeStruct((m, d), BF16)),
        grid=(m // tm,),
        in_specs=[pl.BlockSpec((tm, d), lambda i: (i, 0)),
                  pl.BlockSpec((d, 4 * d), lambda i: (0, 0)),
                  pl.BlockSpec((1, 4 * d), lambda i: (0, 0))],
        out_specs=(pl.BlockSpec((tm, 2 * d), lambda i: (i, 0)), pl.BlockSpec((tm, d), lambda i: (i, 0)),
                   pl.BlockSpec((tm, d), lambda i: (i, 0))),
        compiler_params=_cparams("parallel"),
        name="proj_mix",
    )(x_bf, w_bf, bias)


def _gate_proj_kernel(x_ref, w_ref, b_ref, ga_ref, gm_ref, *, d_model):
    x = x_ref[...]
    d = d_model
    ga_ref[...] = _sigmoid(_dot(x, w_ref[:, 0:d]) + b_ref[:, 0:d]).astype(BF16)
    gm_ref[...] = _sigmoid(_dot(x, w_ref[:, d:2 * d]) + b_ref[:, d:2 * d]).astype(BF16)


def _proj_gates(x_bf, w_bf, bias, tm):
    m, d = x_bf.shape
    return pl.pallas_call(
        functools.partial(_gate_proj_kernel, d_model=d),
        out_shape=(jax.ShapeDtypeStruct((m, d), BF16), jax.ShapeDtypeStruct((m, d), BF16)),
        grid=(m // tm,),
        in_specs=[pl.BlockSpec((tm, d), lambda i: (i, 0)),
                  pl.BlockSpec((d, 2 * d), lambda i: (0, 0)),
                  pl.BlockSpec((1, 2 * d), lambda i: (0, 0))],
        out_specs=(pl.BlockSpec((tm, d), lambda i: (i, 0)), pl.BlockSpec((tm, d), lambda i: (i, 0))),
        compiler_params=_cparams("parallel"),
        name="proj_gates",
    )(x_bf, w_bf, bias)


def _if_proj_kernel(x_ref, wh_ref, wl_ref, b_ref, o_ref, *, n_heads):
    x = x_ref[...]
    xh = x.astype(BF16)
    xl = (x - xh.astype(F32)).astype(BF16)
    wh = wh_ref[...]
    z = _dot(xh, wh) + _dot(xl, wh) + _dot(xh, wl_ref[...]) + b_ref[...]
    lane = lax.broadcasted_iota(jnp.int32, z.shape, 1)
    logsig = jnp.minimum(z, 0.0) - jnp.log(1.0 + jnp.exp(-jnp.abs(z)))
    o_ref[...] = jnp.where(lane < n_heads, z, logsig)


def _proj_if(x_f32, w_if, b_if, tm, n_heads):
    m, d = x_f32.shape
    wpad = jnp.zeros((d, LANE), F32).at[:, :2 * n_heads].set(w_if)
    bpad = jnp.zeros((1, LANE), F32).at[0, :2 * n_heads].set(b_if)
    wh = wpad.astype(BF16)
    wl = (wpad - wh.astype(F32)).astype(BF16)
    return pl.pallas_call(
        functools.partial(_if_proj_kernel, n_heads=n_heads),
        out_shape=jax.ShapeDtypeStruct((m, LANE), F32),
        grid=(m // tm,),
        in_specs=[pl.BlockSpec((tm, d), lambda i: (i, 0)),
                  pl.BlockSpec((d, LANE), lambda i: (0, 0)),
                  pl.BlockSpec((d, LANE), lambda i: (0, 0)),
                  pl.BlockSpec((1, LANE), lambda i: (0, 0))],
        out_specs=pl.BlockSpec((tm, LANE), lambda i: (i, 0)),
        compiler_params=_cparams("parallel"),
        name="proj_if",
    )(x_f32, wh, wl, bpad)


def _flash_update(idx, q, k, v, mask, m_ref, l_ref, acc_ref):
    s = _dot_nt(q, k)
    if mask is not None:
        s = jnp.where(mask, s, NEG)
    reps = s.shape[1] // LANE
    m_prev = m_ref[idx]
    m_new = jnp.maximum(m_prev, jnp.max(s, axis=1, keepdims=True))
    alpha = jnp.exp(m_prev - m_new)
    p = jnp.exp(s - jnp.tile(m_new, (1, reps)))
    l_ref[idx] = alpha * l_ref[idx] + jnp.sum(p, axis=1, keepdims=True)
    acc_ref[idx] = alpha * acc_ref[idx] + _dot(p.astype(BF16), v)
    m_ref[idx] = m_new


def _attn_kernel(qi_tab, kj_tab, qlo_ref, qhi_ref, k_ref, v_ref, km_ref, vm_ref, lam_ref, g_ref,
                 o_ref, m_ref, l_ref, acc_ref, *, n_heads, lam_init):
    s_idx = pl.program_id(1)
    qi = qi_tab[s_idx]
    kj = kj_tab[s_idx]
    tq = qlo_ref.shape[1]
    tk = k_ref.shape[1]

    @pl.when(kj < 0)
    def _meta():
        m_ref[...] = jnp.full(m_ref.shape, NEG, F32)
        l_ref[...] = jnp.zeros(l_ref.shape, F32)
        acc_ref[...] = jnp.zeros(acc_ref.shape, F32)
        mask = lax.broadcasted_iota(jnp.int32, (tq, km_ref.shape[1]), 1) < N_META

        def body(h, c):
            _flash_update(2 * h, qlo_ref[h], km_ref[h], vm_ref[h], mask, m_ref, l_ref, acc_ref)
            _flash_update(2 * h + 1, qhi_ref[h], km_ref[h], vm_ref[h], mask, m_ref, l_ref, acc_ref)
            return c
        lax.fori_loop(0, n_heads, body, 0)

    @pl.when(jnp.logical_and(kj >= 0, kj < qi))
    def _full():
        def body(h, c):
            _flash_update(2 * h, qlo_ref[h], k_ref[h], v_ref[h], None, m_ref, l_ref, acc_ref)
            _flash_update(2 * h + 1, qhi_ref[h], k_ref[h], v_ref[h], None, m_ref, l_ref, acc_ref)
            return c
        lax.fori_loop(0, n_heads, body, 0)

    @pl.when(kj == qi)
    def _diag():
        row = lax.broadcasted_iota(jnp.int32, (tq, tk), 0)
        col = lax.broadcasted_iota(jnp.int32, (tq, tk), 1)
        mask = col <= row
        lam = lam_ref[...]
        g = g_ref[...] * (1.0 - lam_init)

        def body(h, c):
            _flash_update(2 * h, qlo_ref[h], k_ref[h], v_ref[h], mask, m_ref, l_ref, acc_ref)
            _flash_update(2 * h + 1, qhi_ref[h], k_ref[h], v_ref[h], mask, m_ref, l_ref, acc_ref)
            o = acc_ref[2 * h] / l_ref[2 * h] - lam * (acc_ref[2 * h + 1] / l_ref[2 * h + 1])
            o = o * lax.rsqrt(jnp.mean(o * o, axis=1, keepdims=True) + EPS) * g
            o_ref[h] = o.astype(BF16)
            return c
        lax.fori_loop(0, n_heads, body, 0)


def _prompt_attention(qlo, qhi, kb, vb, kmeta, vmeta, lam_row, g_row, batch, lam_init, tile):
    n_heads, m, _ = qlo.shape
    seq = m // batch
    nq = seq // tile
    qi_list, kj_list = [], []
    for qi in range(nq):
        for kj in range(-1, qi + 1):
            qi_list.append(qi)
            kj_list.append(kj)
    qi_tab = jnp.asarray(qi_list, jnp.int32)
    kj_tab = jnp.asarray(kj_list, jnp.int32)
    n_steps = len(qi_list)
    tmeta = kmeta.shape[1]
    qspec = pl.BlockSpec((n_heads, tile, LANE), lambda b, s, qt, kt: (0, b * nq + qt[s], 0))
    kspec = pl.BlockSpec((n_heads, tile, LANE), lambda b, s, qt, kt: (0, b * nq + jnp.maximum(kt[s], 0), 0))
    mspec = pl.BlockSpec((n_heads, tmeta, LANE), lambda b, s, qt, kt: (0, 0, 0))
    rspec = pl.BlockSpec((1, LANE), lambda b, s, qt, kt: (0, 0))
    grid_spec = pltpu.PrefetchScalarGridSpec(
        num_scalar_prefetch=2,
        grid=(batch, n_steps),
        in_specs=[qspec, qspec, kspec, kspec, mspec, mspec, rspec, rspec],
        out_specs=qspec,
        scratch_shapes=[pltpu.VMEM((2 * n_heads, tile, LANE), F32),
                        pltpu.VMEM((2 * n_heads, tile, LANE), F32),
                        pltpu.VMEM((2 * n_heads, tile, LANE), F32)],
    )
    return pl.pallas_call(
        functools.partial(_attn_kernel, n_heads=n_heads, lam_init=lam_init),
        out_shape=jax.ShapeDtypeStruct((n_heads, m, LANE), BF16),
        grid_spec=grid_spec,
        compiler_params=_cparams("parallel", "arbitrary"),
        name="prompt_attention",
    )(qi_tab, kj_tab, qlo, qhi, kb, vb, kmeta, vmeta, lam_row, g_row)


def _paged_kernel(pt_ref, q_ref, *refs, n_heads, n_tok, pages, page, lam_init):
    k_refs = refs[:pages]
    v_refs = refs[pages:2 * pages]
    kn_ref, vn_ref, lam_ref, g_ref, o_ref, qbd_ref, kb_ref, vb_ref, m_ref, l_ref, acc_ref = refs[2 * pages:]
    j = pl.program_id(1)
    rows = 2 * n_heads * n_tok
    d = q_ref.shape[1]

    @pl.when(j == 0)
    def _init():
        q = q_ref[...]
        qt = jnp.tile(q, (rows // n_tok, 1))
        r = lax.broadcasted_iota(jnp.int32, (rows, d), 0)
        c = lax.broadcasted_iota(jnp.int32, (rows, d), 1)
        qbd_ref[...] = jnp.where((c // D_QK) == (r // n_tok), qt, 0.0).astype(BF16)
        m_ref[...] = jnp.full(m_ref.shape, NEG, F32)
        l_ref[...] = jnp.zeros(l_ref.shape, F32)
        acc_ref[...] = jnp.zeros(acc_ref.shape, F32)

    def update(k, v, mask):
        s = _dot_nt(qbd_ref[...], k)
        if mask is not None:
            s = jnp.where(mask, s, NEG)
        m_prev = m_ref[...]
        m_new = jnp.maximum(m_prev, jnp.max(s, axis=1, keepdims=True))
        alpha = jnp.exp(m_prev - m_new)
        p = jnp.exp(s - jnp.tile(m_new, (1, s.shape[1] // LANE)))
        l_ref[...] = alpha * l_ref[...] + jnp.sum(p, axis=1, keepdims=True)
        acc_ref[...] = jnp.tile(alpha, (1, d // LANE)) * acc_ref[...] + _dot(p.astype(BF16), v)
        m_ref[...] = m_new

    for p in range(pages):
        for h in range(n_heads):
            kb_ref[p * page:(p + 1) * page, h * LANE:(h + 1) * LANE] = (
                k_refs[p][pl.ds(h, page, stride=n_heads), :].astype(BF16))
            vb_ref[p * page:(p + 1) * page, h * LANE:(h + 1) * LANE] = (
                v_refs[p][pl.ds(h, page, stride=n_heads), :].astype(BF16))
    update(kb_ref[...], vb_ref[...], None)

    @pl.when(j == pl.num_programs(1) - 1)
    def _final():
        r = lax.broadcasted_iota(jnp.int32, (rows, kn_ref.shape[0]), 0)
        c = lax.broadcasted_iota(jnp.int32, (rows, kn_ref.shape[0]), 1)
        update(kn_ref[...], vn_ref[...], c <= (r % n_tok))
        o = acc_ref[...] / jnp.tile(l_ref[...], (1, d // LANE))
        lam = lam_ref[...]
        g = g_ref[...] * (1.0 - lam_init)
        for h in range(n_heads):
            r0 = 2 * h * n_tok
            o1 = o[r0:r0 + n_tok, h * LANE:(h + 1) * LANE]
            o2 = o[r0 + n_tok:r0 + 2 * n_tok, h * LANE:(h + 1) * LANE]
            oh = o1 - lam * o2
            oh = oh * lax.rsqrt(jnp.mean(oh * oh, axis=1, keepdims=True) + EPS) * g
            o_ref[:, h * LANE:(h + 1) * LANE] = oh


def _paged_attention(q, knew, vnew, cache_k, cache_v, page_table, lam_row, g_row, n_heads, lam_init, pages):
    bd, n_tok, d = q.shape
    page = cache_k.shape[1] // n_heads
    n_pages = page_table.shape[1]
    rows = 2 * n_heads * n_tok

    def page_spec(p):
        return pl.BlockSpec((None, page * n_heads, LANE),
                            lambda b, j, pt: (pt[b * n_pages + j * pages + p], 0, 0))

    bspec = pl.BlockSpec((None, n_tok, d), lambda b, j, pt: (b, 0, 0))
    nspec = pl.BlockSpec((None, knew.shape[1], d), lambda b, j, pt: (b, 0, 0))
    rspec = pl.BlockSpec((1, LANE), lambda b, j, pt: (0, 0))
    grid_spec = pltpu.PrefetchScalarGridSpec(
        num_scalar_prefetch=1,
        grid=(bd, n_pages // pages),
        in_specs=[bspec] + [page_spec(p) for p in range(pages)] * 2 + [nspec, nspec, rspec, rspec],
        out_specs=bspec,
        scratch_shapes=[pltpu.VMEM((rows, d), BF16),
                        pltpu.VMEM((pages * page, d), BF16),
                        pltpu.VMEM((pages * page, d), BF16),
                        pltpu.VMEM((rows, LANE), F32),
                        pltpu.VMEM((rows, LANE), F32),
                        pltpu.VMEM((rows, d), F32)],
    )
    args = [page_table.reshape(-1), q] + [cache_k] * pages + [cache_v] * pages + [knew, vnew, lam_row, g_row]
    return pl.pallas_call(
        functools.partial(_paged_kernel, n_heads=n_heads, n_tok=n_tok, pages=pages, page=page,
                          lam_init=lam_init),
        out_shape=jax.ShapeDtypeStruct((bd, n_tok, d), F32),
        grid_spec=grid_spec,
        compiler_params=_cparams("parallel", "arbitrary"),
        name="paged_attention",
    )(*args)


def _mlstm_kernel(qk_ref, v_ref, om_ref, gcol_ref, grow_ref, cinit_ref, c0_ref, n0_ref, m0_ref,
                  cw_ref, cb_ref, gn_ref, y_ref, c_ref, n_ref, m_ref, xpad_ref, *, n_heads):
    ci = pl.program_id(1)
    t = qk_ref.shape[0]
    dm = v_ref.shape[1]
    dh = dm // n_heads
    hist = CONV_W - 1

    @pl.when(ci == 0)
    def _init():
        xpad_ref[0:8, :] = cinit_ref[...]
        c_ref[...] = c0_ref[...]
        n_ref[...] = n0_ref[...]
        m_ref[...] = m0_ref[...]

    x = qk_ref[...]
    xpad_ref[8:8 + t, :] = x
    y = cb_ref[...] + cw_ref[hist:hist + 1, :] * x
    for j in range(hist):
        y = y + cw_ref[j:j + 1, :] * xpad_ref[8 - hist + j:8 - hist + j + t, :]
    xpad_ref[0:8, :] = x[t - 8:t, :]
    y = y * _sigmoid(y)
    q_all = y[:, :dm].astype(BF16)
    k_all = (y[:, dm:] * (dh ** -0.5)).astype(BF16)

    gcol = gcol_ref[...]
    grow = grow_ref[...]
    r = lax.broadcasted_iota(jnp.int32, (t, t), 0)
    c = lax.broadcasted_iota(jnp.int32, (t, t), 1)
    causal = c <= r
    tril = jnp.where(causal, 1.0, 0.0).astype(F32)
    triu = jnp.where(r <= c, 1.0, 0.0).astype(F32)
    bc_all = jnp.dot(tril, gcol, preferred_element_type=F32, precision=lax.Precision.HIGHEST)
    br_all = jnp.dot(grow, triu, preferred_element_type=F32, precision=lax.Precision.HIGHEST)

    for h in range(n_heads):
        sl = slice(h * dh, (h + 1) * dh)
        qh, kh, vh = q_all[:, sl], k_all[:, sl], v_ref[:, sl]
        m_prev = m_ref[h][:, 0:1]
        b_c = bc_all[:, n_heads + h:n_heads + h + 1]
        li_c = gcol[:, h:h + 1]
        b_r = br_all[n_heads + h:n_heads + h + 1, :]
        li_r = grow[h:h + 1, :]
        log_d = jnp.where(causal, b_c - b_r + li_r, NEG)
        m_t = jnp.maximum(b_c + m_prev, jnp.max(log_d, axis=1, keepdims=True))
        s = _dot_nt(qh, kh) * jnp.exp(log_d - m_t)
        w_inter = jnp.exp(b_c + m_prev - m_t)
        ch = c_ref[h]
        num = _dot(s.astype(BF16), vh) + w_inter * _dot_nt(qh, ch.astype(BF16))
        nh = n_ref[h]
        den = jnp.sum(s, axis=1, keepdims=True) + w_inter * jnp.sum(qh.astype(F32) * nh, axis=1, keepdims=True)
        hout = num / jnp.maximum(jnp.abs(den), jnp.exp(-m_t))
        hn = hout * lax.rsqrt(jnp.mean(hout * hout, axis=1, keepdims=True) + EPS) * gn_ref[:, sl]
        y_ref[:, sl] = (_sigmoid(om_ref[:, sl].astype(F32)) * hn).astype(BF16)
        b_last = b_r[:, t - 1:t]
        log_w_r = b_last - b_r + li_r
        m_new = jnp.maximum(b_last + m_prev, jnp.max(log_w_r, axis=1, keepdims=True))
        w_c = jnp.exp(b_last - b_c + li_c - m_new)
        decay = jnp.exp(b_last + m_prev - m_new)
        wv = (w_c * vh.astype(F32)).astype(BF16)
        c_ref[h] = decay * ch + _dot_tn(wv, kh)
        n_ref[h] = decay * nh + jnp.sum(w_c * kh.astype(F32), axis=0, keepdims=True)
        m_ref[h] = jnp.broadcast_to(m_new, (1, LANE))


def _mlstm(qk, v, om, gcol, grow, conv_init, c0, n0, m0, conv_w, conv_b, g_norm, chunk, shared_init):
    bs, seq, dm2 = qk.shape
    dm = dm2 // 2
    n_heads, dh = c0.shape[1], c0.shape[2]
    ng = gcol.shape[2]
    nc = seq // chunk
    sidx = (lambda b: 0) if shared_init else (lambda b: b)
    row = lambda w: pl.BlockSpec((None, chunk, w), lambda b, i: (b, i, 0))
    return pl.pallas_call(
        functools.partial(_mlstm_kernel, n_heads=n_heads),
        out_shape=(jax.ShapeDtypeStruct((bs, seq, dm), BF16),
                   jax.ShapeDtypeStruct((bs, n_heads, dh, dh), F32),
                   jax.ShapeDtypeStruct((bs, n_heads, 1, dh), F32),
                   jax.ShapeDtypeStruct((bs, n_heads, 1, LANE), F32)),
        grid=(bs, nc),
        in_specs=[row(dm2), row(dm), row(dm), row(ng),
                  pl.BlockSpec((None, ng, chunk), lambda b, i: (b, 0, i)),
                  pl.BlockSpec((None, 8, dm2), lambda b, i: (sidx(b), 0, 0)),
                  pl.BlockSpec((None, n_heads, dh, dh), lambda b, i: (sidx(b), 0, 0, 0)),
                  pl.BlockSpec((None, n_heads, 1, dh), lambda b, i: (sidx(b), 0, 0, 0)),
                  pl.BlockSpec((None, n_heads, 1, LANE), lambda b, i: (sidx(b), 0, 0, 0)),
                  pl.BlockSpec((CONV_W, dm2), lambda b, i: (0, 0)),
                  pl.BlockSpec((1, dm2), lambda b, i: (0, 0)),
                  pl.BlockSpec((1, dm), lambda b, i: (0, 0))],
        out_specs=(row(dm),
                   pl.BlockSpec((None, n_heads, dh, dh), lambda b, i: (b, 0, 0, 0)),
                   pl.BlockSpec((None, n_heads, 1, dh), lambda b, i: (b, 0, 0, 0)),
                   pl.BlockSpec((None, n_heads, 1, LANE), lambda b, i: (b, 0, 0, 0))),
        scratch_shapes=[pltpu.VMEM((chunk + 8, dm2), F32)],
        compiler_params=_cparams("parallel", "arbitrary"),
        name="mlstm",
    )(qk, v, om, gcol, grow, conv_init, c0, n0, m0, conv_w, conv_b, g_norm)


def _layer_norm(z, g, b):
    mu = jnp.mean(z, axis=1, keepdims=True)
    zc = z - mu
    var = jnp.mean(zc * zc, axis=1, keepdims=True)
    return zc * lax.rsqrt(var + EPS) * g + b


def _merge_kernel(x_ref, ya_ref, ym_ref, ga_ref, gm_ref, wba_ref, wbm_ref, wout_ref, g_ref, b_ref,
                  wrh_ref, wrl_ref, h_ref, hrow_ref, s_ref, *, head_major):
    tm, d = x_ref.shape
    if head_major:
        ya = jnp.concatenate([ya_ref[h] for h in range(ya_ref.shape[0])], axis=1)
    else:
        ya = ya_ref[...]
    a = _dot(ya, wba_ref[...])
    m = _dot(ym_ref[...], wbm_ref[...])
    mix = ga_ref[...].astype(F32) * a + gm_ref[...].astype(F32) * m
    z = ALPHA * x_ref[...] + _dot(mix.astype(BF16), wout_ref[...])
    h = _layer_norm(z, g_ref[...], b_ref[...])
    h_ref[...] = h
    n_chunk = d // LANE
    for c in range(n_chunk):
        hrow_ref[pl.ds(c, tm, stride=n_chunk), :] = h[:, c * LANE:(c + 1) * LANE]
    hh = h.astype(BF16)
    hl = (h - hh.astype(F32)).astype(BF16)
    wrh = wrh_ref[...]
    s_ref[...] = _sigmoid(_dot(hh, wrh) + _dot(hl, wrh) + _dot(hh, wrl_ref[...]))


def _merge(x, ya, ym, ga, gm, wba, wbm, wout, g, b, wrh, wrl, tm, head_major):
    m, d = x.shape
    ne = wrh.shape[1]
    row = pl.BlockSpec((tm, d), lambda i: (i, 0))
    full = lambda a: pl.BlockSpec(a.shape, lambda i: (0,) * a.ndim)
    ya_spec = pl.BlockSpec((ya.shape[0], tm, LANE), lambda i: (0, i, 0)) if head_major else row
    return pl.pallas_call(
        functools.partial(_merge_kernel, head_major=head_major),
        out_shape=(jax.ShapeDtypeStruct((m, d), F32), jax.ShapeDtypeStruct((m * (d // LANE), LANE), F32),
                   jax.ShapeDtypeStruct((m, ne), F32)),
        grid=(m // tm,),
        in_specs=[row, ya_spec, row, row, row, full(wba), full(wbm), full(wout), full(g), full(b),
                  full(wrh), full(wrl)],
        out_specs=(row, pl.BlockSpec((tm * (d // LANE), LANE), lambda i: (i, 0)),
                   pl.BlockSpec((tm, ne), lambda i: (i, 0))),
        compiler_params=_cparams("parallel"),
        name="merge_ln_router",
    )(x, ya, ym, ga, gm, wba, wbm, wout, g, b, wrh, wrl)


def _route_kernel(s_ref, b_ref, idx_ref, rank_ref, gate_ref, cnt_ref, carry_ref, *, n_exp):
    i = pl.program_id(0)

    @pl.when(i == 0)
    def _init():
        carry_ref[...] = jnp.zeros(carry_ref.shape, F32)

    s = s_ref[...]
    tm = s.shape[0]
    ssel = s + b_ref[...]
    gsz = n_exp // N_GROUPS
    ninf = -jnp.inf
    lane_i = lax.broadcasted_iota(jnp.int32, (tm, n_exp), 1)
    lane = lane_i.astype(F32)
    lgrp = lane_i // gsz
    slot = lax.broadcasted_iota(jnp.int32, (tm, LANE), 1)

    grp = []
    for g in range(N_GROUPS):
        mg = jnp.where(lgrp == g, ssel, ninf)
        m1 = jnp.max(mg, axis=1, keepdims=True)
        i1 = jnp.min(jnp.where(mg == m1, lane, float(n_exp)), axis=1, keepdims=True)
        m2 = jnp.max(jnp.where(lane == i1, ninf, mg), axis=1, keepdims=True)
        grp.append(m1 + m2)
    enabled = jnp.zeros((tm, n_exp), F32)
    for g in range(N_GROUPS):
        ahead = jnp.zeros((tm, 1), F32)
        for j in range(N_GROUPS):
            if j != g:
                beats = (grp[j] > grp[g]) if j > g else (grp[j] >= grp[g])
                ahead = ahead + jnp.where(beats, 1.0, 0.0)
        keep = jnp.where(ahead < TOPK_GROUPS, 1.0, 0.0)
        enabled = jnp.where(lgrp == g, keep, enabled)
    cur = jnp.where(enabled > 0.5, ssel, ninf)

    sel = jnp.zeros((tm, n_exp), F32)
    idx_out = jnp.zeros((tm, LANE), F32)
    gate_out = jnp.zeros((tm, LANE), F32)
    gsum = jnp.zeros((tm, 1), F32)
    picks = []
    for k in range(TOP_K):
        m = jnp.max(cur, axis=1, keepdims=True)
        ik = jnp.min(jnp.where(cur == m, lane, float(n_exp)), axis=1, keepdims=True)
        pick = lane == ik
        gk = jnp.sum(jnp.where(pick, s, 0.0), axis=1, keepdims=True)
        cur = jnp.where(pick, ninf, cur)
        sel = jnp.where(pick, 1.0, sel)
        idx_out = jnp.where(slot == k, ik, idx_out)
        gate_out = jnp.where(slot == k, gk, gate_out)
        gsum = gsum + gk
        picks.append(ik)
    idx_ref[...] = idx_out
    gate_ref[...] = gate_out / gsum * ROUTE_SCALE

    r = lax.broadcasted_iota(jnp.int32, (tm, tm), 0)
    c = lax.broadcasted_iota(jnp.int32, (tm, tm), 1)
    ltri = jnp.where(c < r, 1.0, 0.0).astype(BF16)
    rank_dense = carry_ref[...] + _dot(ltri, sel.astype(BF16))
    rank_out = jnp.zeros((tm, LANE), F32)
    for k in range(TOP_K):
        rk = jnp.sum(jnp.where(lane == picks[k], rank_dense, 0.0), axis=1, keepdims=True)
        rank_out = jnp.where(slot == k, rk, rank_out)
    rank_ref[...] = rank_out
    total = carry_ref[...] + jnp.sum(sel, axis=0, keepdims=True)
    carry_ref[...] = total
    cnt_ref[...] = jnp.broadcast_to(total, cnt_ref.shape)


def _route(s, b_row, tm):
    t, n_exp = s.shape
    tok = jax.ShapeDtypeStruct((t, LANE), F32)
    tspec = pl.BlockSpec((tm, LANE), lambda i: (i, 0))
    return pl.pallas_call(
        functools.partial(_route_kernel, n_exp=n_exp),
        out_shape=(tok, tok, tok, jax.ShapeDtypeStruct((8, n_exp), F32)),
        grid=(t // tm,),
        in_specs=[pl.BlockSpec((tm, n_exp), lambda i: (i, 0)), pl.BlockSpec((1, n_exp), lambda i: (0, 0))],
        out_specs=(tspec, tspec, tspec, pl.BlockSpec((8, n_exp), lambda i: (0, 0))),
        scratch_shapes=[pltpu.VMEM((1, n_exp), F32)],
        compiler_params=_cparams("arbitrary"),
        name="route",
    )(s, b_row)


def _rowcopy_kernel(sidx_ref, didx_ref, src_ref, dst_ref, sem, *, n, unroll):
    def issue(g, carry):
        for u in range(unroll):
            j = g * unroll + u
            pltpu.make_async_copy(src_ref.at[sidx_ref[0, j]], dst_ref.at[didx_ref[0, j]], sem).start()
        return carry
    lax.fori_loop(0, n // unroll, issue, 0)

    def drain(g, carry):
        for u in range(unroll):
            pltpu.make_async_copy(src_ref.at[0], dst_ref.at[0], sem).wait()
        return carry
    lax.fori_loop(0, n // unroll, drain, 0)


def _copy_rows(src, sidx, didx, n_dst, per_step):
    n = sidx.shape[0]
    steps = n // per_step
    ispec = pl.BlockSpec((None, 1, per_step), lambda i: (i, 0, 0), memory_space=pltpu.SMEM)
    return pl.pallas_call(
        functools.partial(_rowcopy_kernel, n=per_step, unroll=math.gcd(per_step, 8)),
        out_shape=jax.ShapeDtypeStruct((n_dst,) + src.shape[1:], src.dtype),
        grid=(steps,),
        in_specs=[ispec, ispec, pl.BlockSpec(memory_space=pl.ANY)],
        out_specs=pl.BlockSpec(memory_space=pl.ANY),
        scratch_shapes=[pltpu.SemaphoreType.DMA(())],
        compiler_params=_cparams("arbitrary"),
        name="copy_rows",
    )(sidx.reshape(steps, 1, per_step), didx.reshape(steps, 1, per_step), src)


def _expert_kernel(win_ref, exp_ref, lo_ref, hi_ref, x_ref, w1_ref, w3_ref, w2_ref, y_ref,
                   w1b_ref, w3b_ref, w2b_ref, xb_ref, *, blk, n_chunk):
    s = pl.program_id(0)
    sp = jnp.maximum(s - 1, 0)
    new_win = jnp.logical_or(s == 0, win_ref[s] != win_ref[sp])
    new_exp = jnp.logical_or(s == 0, exp_ref[s] != exp_ref[sp])
    lo = lo_ref[s]
    hi = hi_ref[s]

    @pl.when(new_win)
    def _load_rows():
        for c in range(n_chunk):
            xb_ref[:, c * LANE:(c + 1) * LANE] = x_ref[pl.ds(c, blk, stride=n_chunk), :].astype(BF16)
        y_ref[...] = jnp.zeros(y_ref.shape, F32)

    @pl.when(new_exp)
    def _cast():
        w1b_ref[...] = w1_ref[...].astype(BF16)
        w3b_ref[...] = w3_ref[...].astype(BF16)
        w2b_ref[...] = w2_ref[...].astype(BF16)

    @pl.when(hi > lo)
    def _run():
        x = xb_ref[...]
        a = _dot(x, w1b_ref[...])
        g = (a * _sigmoid(a)) * _dot(x, w3b_ref[...])
        y = _dot(g.astype(BF16), w2b_ref[...])
        row = lax.broadcasted_iota(jnp.int32, (blk, LANE), 0)
        mine = jnp.logical_and(row >= lo, row < hi)
        for c in range(n_chunk):
            sl = pl.ds(c, blk, stride=n_chunk)
            y_ref[sl, :] = jnp.where(mine, y[:, c * LANE:(c + 1) * LANE], y_ref[sl, :])


def _experts(xrow, win, exp, lo, hi, w1, w3, w2, blk):
    d, de = w1.shape[1], w1.shape[2]
    n_chunk = d // LANE
    n_steps = win.shape[0]
    xspec = pl.BlockSpec((blk * n_chunk, LANE), lambda s, wn, ex, l, h: (wn[s], 0))
    grid_spec = pltpu.PrefetchScalarGridSpec(
        num_scalar_prefetch=4,
        grid=(n_steps,),
        in_specs=[xspec,
                  pl.BlockSpec((None, d, de), lambda s, wn, ex, l, h: (ex[s], 0, 0)),
                  pl.BlockSpec((None, d, de), lambda s, wn, ex, l, h: (ex[s], 0, 0)),
                  pl.BlockSpec((None, de, d), lambda s, wn, ex, l, h: (ex[s], 0, 0))],
        out_specs=xspec,
        scratch_shapes=[pltpu.VMEM((d, de), BF16), pltpu.VMEM((d, de), BF16), pltpu.VMEM((de, d), BF16),
                        pltpu.VMEM((blk, d), BF16)],
    )
    return pl.pallas_call(
        functools.partial(_expert_kernel, blk=blk, n_chunk=n_chunk),
        out_shape=jax.ShapeDtypeStruct(xrow.shape, F32),
        grid_spec=grid_spec,
        compiler_params=_cparams("arbitrary"),
        name="experts",
    )(win, exp, lo, hi, xrow, w1, w3, w2)


def _final_kernel(h_ref, yg_ref, gate_ref, ws1_ref, ws3_ref, ws2_ref, g_ref, b_ref, o_ref, *, n_chunk):
    h = h_ref[...]
    tm = h.shape[0]
    x = h.astype(BF16)
    a = _dot(x, ws1_ref[...])
    gg = (a * _sigmoid(a)) * _dot(x, ws3_ref[...])
    shared = _dot(gg.astype(BF16), ws2_ref[...])
    gate = gate_ref[...]
    gk = [jnp.broadcast_to(gate[:, k:k + 1], (tm, LANE)) for k in range(TOP_K)]
    stride = TOP_K * n_chunk
    chunks = []
    for c in range(n_chunk):
        acc = gk[0] * yg_ref[pl.ds(c, tm, stride=stride), :]
        for k in range(1, TOP_K):
            acc = acc + gk[k] * yg_ref[pl.ds(k * n_chunk + c, tm, stride=stride), :]
        chunks.append(acc)
    f = jnp.concatenate(chunks, axis=1) + shared
    o_ref[...] = _layer_norm(ALPHA * h + f, g_ref[...], b_ref[...])


def _final(h, yg, gate, ws1, ws3, ws2, g, b, tm, blk_off):
    m, d = h.shape
    n_chunk = d // LANE
    row = pl.BlockSpec((tm, d), lambda i: (i, 0))
    full = lambda a: pl.BlockSpec(a.shape, lambda i: (0,) * a.ndim)
    return pl.pallas_call(
        functools.partial(_final_kernel, n_chunk=n_chunk),
        out_shape=jax.ShapeDtypeStruct((m, d), F32),
        grid=(m // tm,),
        in_specs=[row, pl.BlockSpec((tm * TOP_K * n_chunk, LANE), lambda i: (i + blk_off, 0)),
                  pl.BlockSpec((tm, LANE), lambda i: (i + blk_off, 0)),
                  full(ws1), full(ws3), full(ws2), full(g), full(b)],
        out_specs=row,
        compiler_params=_cparams("parallel"),
        name="shared_ln",
    )(h, yg, gate, ws1, ws3, ws2, g, b)


def _rope_tables(pos):
    half = ROPE_DIM // 2
    inv = ROPE_THETA ** (-jnp.arange(half, dtype=F32) * 2.0 / ROPE_DIM)
    ang = pos.astype(F32)[:, None] * inv
    cos8, sin8 = jnp.cos(ang), jnp.sin(ang)
    n = pos.shape[0]
    cos = jnp.concatenate([cos8, cos8, jnp.ones((n, D_QK - ROPE_DIM), F32)], axis=1)
    sin = jnp.concatenate([-sin8, sin8, jnp.zeros((n, D_QK - ROPE_DIM), F32)], axis=1)
    return jnp.tile(cos, (1, LANE // D_QK)), jnp.tile(sin, (1, LANE // D_QK))


def _expert_steps(starts, ends, n_win, blk):
    n_exp = starts.shape[0]
    first_w = starts // blk
    n_w = jnp.where(ends > starts, (ends + blk - 1) // blk - first_w, 0)
    step_end = jnp.cumsum(n_w)
    step_start = step_end - n_w
    total = step_end[-1]
    s = jnp.arange(n_win + n_exp, dtype=jnp.int32)
    e = jnp.minimum(jnp.searchsorted(step_end, s, side='right'), n_exp - 1).astype(jnp.int32)
    w = first_w[e] + (s - step_start[e])
    lo = jnp.maximum(starts[e], w * blk) - w * blk
    hi = jnp.minimum(ends[e], (w + 1) * blk) - w * blk
    valid = s < total
    last = jnp.maximum(total - 1, 0)
    e = jnp.where(valid, e, e[last])
    w = jnp.where(valid, w, w[last])
    lo = jnp.where(valid, lo, 0)
    hi = jnp.where(valid, hi, 0)
    return w.astype(jnp.int32), e, lo.astype(jnp.int32), hi.astype(jnp.int32)


def _row_tile(m, pref):
    t = pref
    while m % t:
        t //= 2
    return t


def kernel(x_prompt, x_sample, cache_k, cache_v, page_table, state_C, state_n, state_m, state_conv, meta_tokens, w_in, b_in, lam_q1, lam_k1, lam_q2, lam_k2, g_subln, conv_w, conv_b, g_mnorm, w_ba, w_bm, w_out, ln1_g, ln1_b, w_router, b_router, w1, w3, w2, ws1, ws3, ws2, ln2_g, ln2_b):
    B, SEQ, D = x_prompt.shape
    BD, S, _ = x_sample.shape
    NHA, DVA = cache_k.shape[3], cache_v.shape[4]
    NHM, DHM = state_C.shape[2], state_C.shape[3]
    n_exp = w_router.shape[2]
    n_pages = page_table.shape[1]
    page = cache_k.shape[2]
    l = 0
    lam_init = 0.8 - 0.6 * math.exp(-0.3 * l)
    lam = (jnp.exp(jnp.sum(lam_q1[l] * lam_k1[l])) - jnp.exp(jnp.sum(lam_q2[l] * lam_k2[l])) + lam_init)
    lam_row = jnp.broadcast_to(lam.astype(F32), (1, LANE))
    g_row = g_subln[l].astype(F32).reshape(1, LANE)

    w = w_in[l]
    bias = b_in[l].astype(F32)
    o_mix = 3 * D
    o_if = o_mix + 4 * D
    o_g = o_if + 2 * NHM
    w_qkv = w[:, :o_mix].astype(BF16)
    w_mix = w[:, o_mix:o_if].astype(BF16)
    w_g = w[:, o_g:].astype(BF16)
    w_if = w[:, o_if:o_g]
    b_qkv = bias[None, :o_mix]
    b_mix = bias[None, o_mix:o_if]
    b_g = bias[None, o_g:]
    b_if = bias[o_if:o_g]

    MP = B * SEQ
    xp = x_prompt.reshape(MP, D)
    xs = x_sample.reshape(BD * S, D)
    xm = jnp.zeros((PAD_CHUNK, D), F32).at[:N_META].set(meta_tokens.astype(F32))
    tmp = _row_tile(SEQ, 512)
    tms = _row_tile(BD * S, 256)

    cos_p, sin_p = _rope_tables(N_META + jnp.arange(SEQ))
    cos_s, sin_s = _rope_tables(jnp.tile(n_pages * page + jnp.arange(S), BD))
    cos_m, sin_m = _rope_tables(jnp.arange(PAD_CHUNK))

    xp_bf, xs_bf, xm_bf = xp.astype(BF16), xs.astype(BF16), xm.astype(BF16)
    qlo_p, qhi_p, kf_p, kb_p, vf_p, vb_p = _proj_qkv(xp_bf, w_qkv, b_qkv, cos_p, sin_p, tmp, NHA)
    qlo_s, qhi_s, kf_s, _, vf_s, _ = _proj_qkv(xs_bf, w_qkv, b_qkv, cos_s, sin_s, tms, NHA)
    _, _, kf_m, kb_m, vf_m, vb_m = _proj_qkv(xm_bf, w_qkv, b_qkv, cos_m, sin_m, PAD_CHUNK, NHA)

    qk_p, vm_p, om_p = _proj_mix(xp_bf, w_mix, b_mix, tmp)
    qk_s, vm_s, om_s = _proj_mix(xs_bf, w_mix, b_mix, tms)
    qk_m, vm_m, om_m = _proj_mix(xm_bf, w_mix, b_mix, PAD_CHUNK)

    ga_p, gm_p = _proj_gates(xp_bf, w_g, b_g, tmp)
    ga_s, gm_s = _proj_gates(xs_bf, w_g, b_g, tms)

    if_p = _proj_if(xp, w_if, b_if, tmp, NHM)[:, :2 * NHM]
    if_s = _proj_if(xs, w_if, b_if, tms, NHM)[:, :2 * NHM]
    if_m = _proj_if(xm, w_if, b_if, PAD_CHUNK, NHM)[:, :2 * NHM]

    ya_p = _prompt_attention(qlo_p, qhi_p, kb_p, vb_p, kb_m, vb_m, lam_row, g_row, B, lam_init,
                             _row_tile(SEQ, ATTN_TILE))
    q_s = jnp.transpose(qlo_s + qhi_s, (1, 0, 2)).reshape(BD, S, D).astype(F32)
    kn = jnp.zeros((BD, LANE, D), BF16).at[:, :S].set(kf_s.reshape(BD, S, D).astype(BF16))
    vn = jnp.zeros((BD, LANE, D), BF16).at[:, :S].set(vf_s.reshape(BD, S, D).astype(BF16))
    n_pool = cache_k.shape[1]
    ya_s = _paged_attention(q_s, kn, vn, cache_k.reshape(-1, page * NHA, DVA), cache_v.reshape(-1, page * NHA, DVA),
                            page_table + l * n_pool, lam_row, g_row, NHA, lam_init,
                            math.gcd(PAGES_PER_STEP, n_pages))

    cw = conv_w[l].astype(F32)
    cb = conv_b[l].astype(F32)[None]
    gn = g_mnorm[l].astype(F32)[None]
    pad_gate = jnp.concatenate([jnp.full((NHM,), NEG, F32), jnp.zeros((NHM,), F32)])

    def gate_layouts(g, bs, real, total):
        g = g.reshape(bs, -1, 2 * NHM)[:, :real]
        if total > real:
            g = jnp.concatenate([g, jnp.broadcast_to(pad_gate, (bs, total - real, 2 * NHM))], axis=1)
        return g, jnp.transpose(g, (0, 2, 1))

    def pad_rows(a, bs, real, total):
        a = a.reshape(bs, -1, a.shape[-1])[:, :real]
        return jnp.pad(a, ((0, 0), (0, total - real), (0, 0)))

    gcol_m, grow_m = gate_layouts(if_m, 1, N_META, PAD_CHUNK)
    zC = jnp.zeros((1, NHM, DHM, DHM), F32)
    zn = jnp.zeros((1, NHM, 1, DHM), F32)
    zm = jnp.zeros((1, NHM, 1, LANE), F32)
    _, c_m, n_m, m_m = _mlstm(pad_rows(qk_m, 1, N_META, PAD_CHUNK), pad_rows(vm_m, 1, N_META, PAD_CHUNK),
                              pad_rows(om_m, 1, N_META, PAD_CHUNK), gcol_m, grow_m,
                              jnp.zeros((1, 8, 2 * D), F32), zC, zn, zm, cw, cb, gn, PAD_CHUNK, True)
    hist_m = jnp.zeros((1, 8, 2 * D), F32).at[0, 8 - (CONV_W - 1):].set(qk_m[N_META - (CONV_W - 1):N_META])
    gcol_p, grow_p = gate_layouts(if_p, B, SEQ, SEQ)
    ym_p, c_p, n_p, m_p = _mlstm(qk_p.reshape(B, SEQ, 2 * D), vm_p.reshape(B, SEQ, D), om_p.reshape(B, SEQ, D),
                                 gcol_p, grow_p, hist_m, c_m, n_m, m_m, cw, cb, gn,
                                 _row_tile(SEQ, MLSTM_CHUNK), True)
    gcol_s, grow_s = gate_layouts(if_s, BD, S, PAD_CHUNK)
    hist_s = jnp.zeros((BD, 8, 2 * D), F32).at[:, 8 - (CONV_W - 1):].set(state_conv[l].astype(F32))
    ym_s, c_s, n_s, m_s = _mlstm(pad_rows(qk_s, BD, S, PAD_CHUNK), pad_rows(vm_s, BD, S, PAD_CHUNK),
                                 pad_rows(om_s, BD, S, PAD_CHUNK), gcol_s, grow_s, hist_s,
                                 state_C[l].astype(F32), state_n[l].astype(F32)[:, :, None, :],
                                 jnp.broadcast_to(state_m[l].astype(F32)[:, :, None, None], (BD, NHM, 1, LANE)),
                                 cw, cb, gn, PAD_CHUNK, False)
    ym_s = ym_s[:, :S].reshape(BD * S, D)

    wba, wbm, wout = w_ba[l].astype(BF16), w_bm[l].astype(BF16), w_out[l].astype(BF16)
    g1, b1 = ln1_g[l].astype(F32)[None], ln1_b[l].astype(F32)[None]
    wr = w_router[l].astype(F32)
    wrh = wr.astype(BF16)
    wrl = (wr - wrh.astype(F32)).astype(BF16)
    h_p, hrow_p, s_p = _merge(xp, ya_p, ym_p.reshape(MP, D), ga_p, gm_p, wba, wbm, wout, g1, b1, wrh, wrl,
                              _row_tile(MP, 256), True)
    h_s, hrow_s, s_s = _merge(xs, ya_s.reshape(BD * S, D).astype(BF16), ym_s, ga_s, gm_s, wba, wbm, wout,
                              g1, b1, wrh, wrl, tms, False)

    MT = MP + BD * S
    n_chunk = D // LANE
    tk = MT * TOP_K
    hrow = jnp.concatenate([hrow_p, hrow_s], axis=0).reshape(MT, n_chunk, LANE)
    s_all = jnp.concatenate([s_p, s_s], axis=0)
    idx_f, rank_f, gate, cnt = _route(s_all, b_router[l].astype(F32)[None], _row_tile(MT, 256))
    idx = idx_f[:, :TOP_K].astype(jnp.int32)
    counts = cnt[0].astype(jnp.int32)
    ends = jnp.cumsum(counts)
    starts = ends - counts
    pos = (starts[idx] + rank_f[:, :TOP_K].astype(jnp.int32)).reshape(tk)
    n_win = -(-tk // MOE_BLOCK)
    per_step = _row_tile(tk, 2048)
    tok = jnp.repeat(jnp.arange(MT, dtype=jnp.int32), TOP_K)
    xrow = _copy_rows(hrow, tok, pos, n_win * MOE_BLOCK, per_step)
    win, exp, lo, hi = _expert_steps(starts, ends, n_win, MOE_BLOCK)
    yrow = _experts(xrow.reshape(-1, LANE), win, exp, lo, hi, w1[l], w3[l], w2[l], MOE_BLOCK)
    yg = _copy_rows(yrow.reshape(-1, n_chunk, LANE), pos, jnp.arange(tk, dtype=jnp.int32), tk, per_step)
    yg = yg.reshape(tk * n_chunk, LANE)

    ws1b, ws3b, ws2b = ws1[l].astype(BF16), ws3[l].astype(BF16), ws2[l].astype(BF16)
    g2, b2 = ln2_g[l].astype(F32)[None], ln2_b[l].astype(F32)[None]
    y_p = _final(h_p, yg, gate, ws1b, ws3b, ws2b, g2, b2, _row_tile(MP, 256), 0)
    y_s = _final(h_s, yg, gate, ws1b, ws3b, ws2b, g2, b2, tms, MP // tms)

    def with_meta(meta_rows, real):
        meta = jnp.broadcast_to(meta_rows[None, :N_META], (B, N_META, D))
        return jnp.concatenate([meta, real.reshape(B, SEQ, D)], axis=1).reshape(1, B, N_META + SEQ, NHA, DVA)

    k_prompt = with_meta(kf_m, kf_p)
    v_prompt = with_meta(vf_m, vf_p)
    conv_prompt = qk_p.reshape(B, SEQ, 2 * D)[:, SEQ - (CONV_W - 1):][None]
    conv_sample = qk_s.reshape(BD, S, 2 * D)[:, S - (CONV_W - 1):][None]
    return (y_p.reshape(B, SEQ, D), y_s.reshape(BD, S, D),
            k_prompt, v_prompt,
            c_p[None], n_p[:, :, 0][None], m_p[:, :, 0, 0][None], conv_prompt,
            kf_s.reshape(1, BD, S, NHA, DVA), vf_s.reshape(1, BD, S, NHA, DVA),
            c_s[None], n_s[:, :, 0][None], m_s[:, :, 0, 0][None], conv_sample)
```

```python
import functools
import math

import jax
import jax.numpy as jnp
from jax import lax
from jax.experimental import pallas as pl
from jax.experimental.pallas import tpu as pltpu

F32 = jnp.float32
BF16 = jnp.bfloat16

DEPTH = 1
N_META = 16
D_QK = 64
ROPE_DIM = D_QK // 4
ROPE_THETA = 500000.0
CONV_W = 4
TOP_K = 8
N_GROUPS = 8
TOPK_GROUPS = 4
ROUTE_SCALE = 2.5
ALPHA = (2 * DEPTH) ** 0.25
EPS = 1e-5
NEG = -1e30

LANE = 128
VMEM_LIMIT = 52 * 1024 * 1024

ATTN_TILE = 512
MLSTM_CHUNK = 256
PAD_CHUNK = 128
PAGES_PER_STEP = 8
MOE_BLOCK = 256


def _cparams(*sem):
    return pltpu.CompilerParams(dimension_semantics=sem, vmem_limit_bytes=VMEM_LIMIT)


def _dot(a, b):
    return jnp.dot(a, b, preferred_element_type=F32)


def _dot_nt(a, b):
    return lax.dot_general(a, b, (((1,), (1,)), ((), ())), preferred_element_type=F32)


def _dot_tn(a, b):
    return lax.dot_general(a, b, (((0,), (0,)), ((), ())), preferred_element_type=F32)


def _split3(a):
    a1 = a.astype(BF16)
    r1 = a - a1.astype(F32)
    a2 = r1.astype(BF16)
    a3 = (r1 - a2.astype(F32)).astype(BF16)
    return a1, a2, a3


def _sigmoid(x):
    return 1.0 / (1.0 + jnp.exp(-x))


def _qkv_kernel(x_ref, w_ref, b_ref, cos_ref, sin_ref,
                qlo_ref, qhi_ref, kf_ref, kb_ref, vf_ref, vb_ref, *, n_heads, d_model):
    x = x_ref[...]
    cos = cos_ref[...]
    sin = sin_ref[...]
    lane = lax.broadcasted_iota(jnp.int32, cos.shape, 1)
    first = (lane % D_QK) < (ROPE_DIM // 2)
    lo_half = lane < D_QK
    half = ROPE_DIM // 2

    def rope(s):
        up = pltpu.roll(s, LANE - half, axis=1)
        dn = pltpu.roll(s, half, axis=1)
        return s * cos + jnp.where(first, up, dn) * sin

    zq = _dot(x, w_ref[:, 0:d_model]) + b_ref[:, 0:d_model]
    for h in range(n_heads):
        q = rope(zq[:, h * LANE:(h + 1) * LANE]) * (D_QK ** -0.5)
        qlo_ref[h] = jnp.where(lo_half, q, 0.0).astype(BF16)
        qhi_ref[h] = jnp.where(lo_half, 0.0, q).astype(BF16)
    zk = _dot(x, w_ref[:, d_model:2 * d_model]) + b_ref[:, d_model:2 * d_model]
    for h in range(n_heads):
        k = rope(zk[:, h * LANE:(h + 1) * LANE])
        kf_ref[:, h * LANE:(h + 1) * LANE] = k
        kb_ref[h] = k.astype(BF16)
    zv = _dot(x, w_ref[:, 2 * d_model:3 * d_model]) + b_ref[:, 2 * d_model:3 * d_model]
    vf_ref[...] = zv
    for h in range(n_heads):
        vb_ref[h] = zv[:, h * LANE:(h + 1) * LANE].astype(BF16)


def _proj_qkv(x_bf, w_bf, bias, cos, sin, tm, n_heads):
    m, d = x_bf.shape
    n_pos = cos.shape[0] // tm
    hm = jax.ShapeDtypeStruct((n_heads, m, LANE), BF16)
    fm = jax.ShapeDtypeStruct((m, d), F32)
    hspec = pl.BlockSpec((n_heads, tm, LANE), lambda i: (0, i, 0))
    fspec = pl.BlockSpec((tm, d), lambda i: (i, 0))
    pspec = pl.BlockSpec((tm, LANE), lambda i: (i % n_pos, 0))
    return pl.pallas_call(
        functools.partial(_qkv_kernel, n_heads=n_heads, d_model=d),
        out_shape=(hm, hm, fm, hm, fm, hm),
        grid=(m // tm,),
        in_specs=[fspec, pl.BlockSpec((d, 3 * d), lambda i: (0, 0)),
                  pl.BlockSpec((1, 3 * d), lambda i: (0, 0)), pspec, pspec],
        out_specs=(hspec, hspec, fspec, hspec, fspec, hspec),
        compiler_params=_cparams("parallel"),
        name="proj_qkv",
    )(x_bf, w_bf, bias, cos, sin)


def _mix_proj_kernel(x_ref, w_ref, b_ref, qk_ref, v_ref, om_ref, *, d_model):
    x = x_ref[...]
    d = d_model
    qk_ref[...] = _dot(x, w_ref[:, 0:2 * d]) + b_ref[:, 0:2 * d]
    v_ref[...] = (_dot(x, w_ref[:, 2 * d:3 * d]) + b_ref[:, 2 * d:3 * d]).astype(BF16)
    om_ref[...] = (_dot(x, w_ref[:, 3 * d:4 * d]) + b_ref[:, 3 * d:4 * d]).astype(BF16)


def _proj_mix(x_bf, w_bf, bias, tm):
    m, d = x_bf.shape
    return pl.pallas_call(
        functools.partial(_mix_proj_kernel, d_model=d),
        out_shape=(jax.ShapeDtypeStruct((m, 2 * d), F32), jax.ShapeDtypeStruct((m, d), BF16),
                   jax.ShapeDtypeStruct((m, d), BF16)),
        grid=(m // tm,),
        in_specs=[pl.BlockSpec((tm, d), lambda i: (i, 0)),
                  pl.BlockSpec((d, 4 * d), lambda i: (0, 0)),
                  pl.BlockSpec((1, 4 * d), lambda i: (0, 0))],
        out_specs=(pl.BlockSpec((tm, 2 * d), lambda i: (i, 0)), pl.BlockSpec((tm, d), lambda i: (i, 0)),
                   pl.BlockSpec((tm, d), lambda i: (i, 0))),
        compiler_params=_cparams("parallel"),
        name="proj_mix",
    )(x_bf, w_bf, bias)


def _gate_proj_kernel(x_ref, w_ref, b_ref, ga_ref, gm_ref, *, d_model):
    x = x_ref[...]
    d = d_model
    ga_ref[...] = _sigmoid(_dot(x, w_ref[:, 0:d]) + b_ref[:, 0:d]).astype(BF16)
    gm_ref[...] = _sigmoid(_dot(x, w_ref[:, d:2 * d]) + b_ref[:, d:2 * d]).astype(BF16)


def _proj_gates(x_bf, w_bf, bias, tm):
    m, d = x_bf.shape
    return pl.pallas_call(
        functools.partial(_gate_proj_kernel, d_model=d),
        out_shape=(jax.ShapeDtypeStruct((m, d), BF16), jax.ShapeDtypeStruct((m, d), BF16)),
        grid=(m // tm,),
        in_specs=[pl.BlockSpec((tm, d), lambda i: (i, 0)),
                  pl.BlockSpec((d, 2 * d), lambda i: (0, 0)),
                  pl.BlockSpec((1, 2 * d), lambda i: (0, 0))],
        out_specs=(pl.BlockSpec((tm, d), lambda i: (i, 0)), pl.BlockSpec((tm, d), lambda i: (i, 0))),
        compiler_params=_cparams("parallel"),
        name="proj_gates",
    )(x_bf, w_bf, bias)


def _if_proj_kernel(x_ref, wh_ref, wl_ref, b_ref, o_ref, *, n_heads):
    x = x_ref[...]
    xh = x.astype(BF16)
    xl = (x - xh.astype(F32)).astype(BF16)
    wh = wh_ref[...]
    z = _dot(xh, wh) + _dot(xl, wh) + _dot(xh, wl_ref[...]) + b_ref[...]
    lane = lax.broadcasted_iota(jnp.int32, z.shape, 1)
    logsig = jnp.minimum(z, 0.0) - jnp.log(1.0 + jnp.exp(-jnp.abs(z)))
    o_ref[...] = jnp.where(lane < n_heads, z, logsig)


def _proj_if(x_f32, w_if, b_if, tm, n_heads):
    m, d = x_f32.shape
    wpad = jnp.zeros((d, LANE), F32).at[:, :2 * n_heads].set(w_if)
    bpad = jnp.zeros((1, LANE), F32).at[0, :2 * n_heads].set(b_if)
    wh = wpad.astype(BF16)
    wl = (wpad - wh.astype(F32)).astype(BF16)
    return pl.pallas_call(
        functools.partial(_if_proj_kernel, n_heads=n_heads),
        out_shape=jax.ShapeDtypeStruct((m, LANE), F32),
        grid=(m // tm,),
        in_specs=[pl.BlockSpec((tm, d), lambda i: (i, 0)),
                  pl.BlockSpec((d, LANE), lambda i: (0, 0)),
                  pl.BlockSpec((d, LANE), lambda i: (0, 0)),
                  pl.BlockSpec((1, LANE), lambda i: (0, 0))],
        out_specs=pl.BlockSpec((tm, LANE), lambda i: (i, 0)),
        compiler_params=_cparams("parallel"),
        name="proj_if",
    )(x_f32, wh, wl, bpad)


def _flash_update(idx, q, kvs, m_ref, l_ref, acc_ref, first):
    ss = []
    for k, _, mask in kvs:
        s = _dot_nt(q, k)
        ss.append(s if mask is None else jnp.where(mask, s, NEG))
    m_cur = functools.reduce(jnp.maximum, [jnp.max(s, axis=1, keepdims=True) for s in ss])
    if first:
        m_new = jnp.broadcast_to(m_cur, (q.shape[0], LANE))
    else:
        m_prev = m_ref[idx]
        m_new = jnp.maximum(m_prev, m_cur)
    lsum, pv = None, None
    for s, (_, v, _) in zip(ss, kvs):
        p = jnp.exp(s - jnp.tile(m_new, (1, s.shape[1] // LANE)))
        ls = jnp.sum(p, axis=1, keepdims=True)
        d = _dot(p.astype(BF16), v)
        lsum = ls if lsum is None else lsum + ls
        pv = d if pv is None else pv + d
    if first:
        l_ref[idx] = jnp.broadcast_to(lsum, (q.shape[0], LANE))
        acc_ref[idx] = pv
    else:
        alpha = jnp.exp(m_prev - m_new)
        l_ref[idx] = alpha * l_ref[idx] + lsum
        acc_ref[idx] = alpha * acc_ref[idx] + pv
    m_ref[idx] = m_new


def _attn_kernel(qi_tab, kj_tab, qlo_ref, qhi_ref, k_ref, v_ref, km_ref, vm_ref, lam_ref, g_ref,
                 o_ref, m_ref, l_ref, acc_ref, *, n_heads, lam_init):
    s_idx = pl.program_id(1)
    qi = qi_tab[s_idx]
    kj = kj_tab[s_idx]
    tq = qlo_ref.shape[1]
    tk = k_ref.shape[1]

    def head_loop(kvs_of, first, finish=None):
        def body(h, c):
            kvs = kvs_of(h)
            _flash_update(2 * h, qlo_ref[h], kvs, m_ref, l_ref, acc_ref, first)
            _flash_update(2 * h + 1, qhi_ref[h], kvs, m_ref, l_ref, acc_ref, first)
            if finish is not None:
                finish(h)
            return c
        lax.fori_loop(0, n_heads, body, 0)

    @pl.when(jnp.logical_and(kj == 0, qi > 0))
    def _first():
        head_loop(lambda h: [(k_ref[h], v_ref[h], None)], True)

    @pl.when(jnp.logical_and(kj > 0, kj < qi))
    def _full():
        head_loop(lambda h: [(k_ref[h], v_ref[h], None)], False)

    @pl.when(kj == qi)
    def _diag():
        @pl.when(qi == 0)
        def _init():
            m_ref[...] = jnp.full(m_ref.shape, NEG, F32)
            l_ref[...] = jnp.zeros(l_ref.shape, F32)
            acc_ref[...] = jnp.zeros(acc_ref.shape, F32)

        row = lax.broadcasted_iota(jnp.int32, (tq, tk), 0)
        col = lax.broadcasted_iota(jnp.int32, (tq, tk), 1)
        causal = col <= row
        is_meta = lax.broadcasted_iota(jnp.int32, (tq, km_ref.shape[1]), 1) < N_META
        lam = lam_ref[...]
        g = g_ref[...] * (1.0 - lam_init)

        def finish(h):
            o = acc_ref[2 * h] / l_ref[2 * h] - lam * (acc_ref[2 * h + 1] / l_ref[2 * h + 1])
            o = o * lax.rsqrt(jnp.mean(o * o, axis=1, keepdims=True) + EPS) * g
            o_ref[h] = o.astype(BF16)

        head_loop(lambda h: [(k_ref[h], v_ref[h], causal), (km_ref[h], vm_ref[h], is_meta)], False, finish)


def _prompt_attention(qlo, qhi, kb, vb, kmeta, vmeta, lam_row, g_row, batch, lam_init, tile):
    n_heads, m, _ = qlo.shape
    seq = m // batch
    nq = seq // tile
    qi_list, kj_list = [], []
    for qi in range(nq):
        for kj in range(qi + 1):
            qi_list.append(qi)
            kj_list.append(kj)
    qi_tab = jnp.asarray(qi_list, jnp.int32)
    kj_tab = jnp.asarray(kj_list, jnp.int32)
    n_steps = len(qi_list)
    tmeta = kmeta.shape[1]
    qspec = pl.BlockSpec((n_heads, tile, LANE), lambda b, s, qt, kt: (0, b * nq + qt[s], 0))
    kspec = pl.BlockSpec((n_heads, tile, LANE), lambda b, s, qt, kt: (0, b * nq + kt[s], 0))
    mspec = pl.BlockSpec((n_heads, tmeta, LANE), lambda b, s, qt, kt: (0, 0, 0))
    rspec = pl.BlockSpec((1, LANE), lambda b, s, qt, kt: (0, 0))
    grid_spec = pltpu.PrefetchScalarGridSpec(
        num_scalar_prefetch=2,
        grid=(batch, n_steps),
        in_specs=[qspec, qspec, kspec, kspec, mspec, mspec, rspec, rspec],
        out_specs=qspec,
        scratch_shapes=[pltpu.VMEM((2 * n_heads, tile, LANE), F32),
                        pltpu.VMEM((2 * n_heads, tile, LANE), F32),
                        pltpu.VMEM((2 * n_heads, tile, LANE), F32)],
    )
    return pl.pallas_call(
        functools.partial(_attn_kernel, n_heads=n_heads, lam_init=lam_init),
        out_shape=jax.ShapeDtypeStruct((n_heads, m, LANE), BF16),
        grid_spec=grid_spec,
        compiler_params=_cparams("parallel", "arbitrary"),
        name="prompt_attention",
    )(qi_tab, kj_tab, qlo, qhi, kb, vb, kmeta, vmeta, lam_row, g_row)


def _paged_kernel(pt_ref, q_ref, *refs, n_heads, n_tok, pages, page, lam_init):
    k_refs = refs[:pages]
    v_refs = refs[pages:2 * pages]
    kn_ref, vn_ref, lam_ref, g_ref, o_ref, qbd_ref, kb_ref, vb_ref, m_ref, l_ref, acc_ref = refs[2 * pages:]
    j = pl.program_id(1)
    rows = 2 * n_heads * n_tok
    d = q_ref.shape[1]

    @pl.when(j == 0)
    def _init():
        q = q_ref[...]
        qt = jnp.tile(q, (rows // n_tok, 1))
        r = lax.broadcasted_iota(jnp.int32, (rows, d), 0)
        c = lax.broadcasted_iota(jnp.int32, (rows, d), 1)
        qbd_ref[...] = jnp.where((c // D_QK) == (r // n_tok), qt, 0.0).astype(BF16)
        m_ref[...] = jnp.full(m_ref.shape, NEG, F32)
        l_ref[...] = jnp.zeros(l_ref.shape, F32)
        acc_ref[...] = jnp.zeros(acc_ref.shape, F32)

    def update(k, v, mask):
        s = _dot_nt(qbd_ref[...], k)
        if mask is not None:
            s = jnp.where(mask, s, NEG)
        m_prev = m_ref[...]
        m_new = jnp.maximum(m_prev, jnp.max(s, axis=1, keepdims=True))
        alpha = jnp.exp(m_prev - m_new)
        p = jnp.exp(s - jnp.tile(m_new, (1, s.shape[1] // LANE)))
        l_ref[...] = alpha * l_ref[...] + jnp.sum(p, axis=1, keepdims=True)
        acc_ref[...] = jnp.tile(alpha, (1, d // LANE)) * acc_ref[...] + _dot(p.astype(BF16), v)
        m_ref[...] = m_new

    for p in range(pages):
        for h in range(n_heads):
            kb_ref[p * page:(p + 1) * page, h * LANE:(h + 1) * LANE] = (
                k_refs[p][pl.ds(h, page, stride=n_heads), :].astype(BF16))
            vb_ref[p * page:(p + 1) * page, h * LANE:(h + 1) * LANE] = (
                v_refs[p][pl.ds(h, page, stride=n_heads), :].astype(BF16))
    update(kb_ref[...], vb_ref[...], None)

    @pl.when(j == pl.num_programs(1) - 1)
    def _final():
        r = lax.broadcasted_iota(jnp.int32, (rows, kn_ref.shape[0]), 0)
        c = lax.broadcasted_iota(jnp.int32, (rows, kn_ref.shape[0]), 1)
        update(kn_ref[...], vn_ref[...], c <= (r % n_tok))
        o = acc_ref[...] / jnp.tile(l_ref[...], (1, d // LANE))
        lam = lam_ref[...]
        g = g_ref[...] * (1.0 - lam_init)
        for h in range(n_heads):
            r0 = 2 * h * n_tok
            o1 = o[r0:r0 + n_tok, h * LANE:(h + 1) * LANE]
            o2 = o[r0 + n_tok:r0 + 2 * n_tok, h * LANE:(h + 1) * LANE]
            oh = o1 - lam * o2
            oh = oh * lax.rsqrt(jnp.mean(oh * oh, axis=1, keepdims=True) + EPS) * g
            o_ref[:, h * LANE:(h + 1) * LANE] = oh


def _paged_attention(q, knew, vnew, cache_k, cache_v, page_table, lam_row, g_row, n_heads, lam_init, pages):
    bd, n_tok, d = q.shape
    page = cache_k.shape[1] // n_heads
    n_pages = page_table.shape[1]
    rows = 2 * n_heads * n_tok

    def page_spec(p):
        return pl.BlockSpec((None, page * n_heads, LANE),
                            lambda b, j, pt: (pt[b * n_pages + j * pages + p], 0, 0))

    bspec = pl.BlockSpec((None, n_tok, d), lambda b, j, pt: (b, 0, 0))
    nspec = pl.BlockSpec((None, knew.shape[1], d), lambda b, j, pt: (b, 0, 0))
    rspec = pl.BlockSpec((1, LANE), lambda b, j, pt: (0, 0))
    grid_spec = pltpu.PrefetchScalarGridSpec(
        num_scalar_prefetch=1,
        grid=(bd, n_pages // pages),
        in_specs=[bspec] + [page_spec(p) for p in range(pages)] * 2 + [nspec, nspec, rspec, rspec],
        out_specs=bspec,
        scratch_shapes=[pltpu.VMEM((rows, d), BF16),
                        pltpu.VMEM((pages * page, d), BF16),
                        pltpu.VMEM((pages * page, d), BF16),
                        pltpu.VMEM((rows, LANE), F32),
                        pltpu.VMEM((rows, LANE), F32),
                        pltpu.VMEM((rows, d), F32)],
    )
    args = [page_table.reshape(-1), q] + [cache_k] * pages + [cache_v] * pages + [knew, vnew, lam_row, g_row]
    return pl.pallas_call(
        functools.partial(_paged_kernel, n_heads=n_heads, n_tok=n_tok, pages=pages, page=page,
                          lam_init=lam_init),
        out_shape=jax.ShapeDtypeStruct((bd, n_tok, d), F32),
        grid_spec=grid_spec,
        compiler_params=_cparams("parallel", "arbitrary"),
        name="paged_attention",
    )(*args)


def _mlstm_kernel(qk_ref, v_ref, om_ref, gcol_ref, grow_ref, cinit_ref, c0_ref, n0_ref, m0_ref,
                  cw_ref, cb_ref, gn_ref, y_ref, c_ref, n_ref, m_ref, xpad_ref, *, n_heads):
    ci = pl.program_id(1)
    t = qk_ref.shape[0]
    dm = v_ref.shape[1]
    dh = dm // n_heads
    hist = CONV_W - 1

    @pl.when(ci == 0)
    def _init():
        xpad_ref[0:8, :] = cinit_ref[...]
        c_ref[...] = c0_ref[...]
        n_ref[...] = n0_ref[...]
        m_ref[...] = m0_ref[...]

    x = qk_ref[...]
    xpad_ref[8:8 + t, :] = x
    y = cb_ref[...] + cw_ref[hist:hist + 1, :] * x
    for j in range(hist):
        y = y + cw_ref[j:j + 1, :] * xpad_ref[8 - hist + j:8 - hist + j + t, :]
    xpad_ref[0:8, :] = x[t - 8:t, :]
    y = y * _sigmoid(y)
    q_all = y[:, :dm].astype(BF16)
    k_all = (y[:, dm:] * (dh ** -0.5)).astype(BF16)

    gcol = gcol_ref[...]
    grow = grow_ref[...]
    r = lax.broadcasted_iota(jnp.int32, (t, t), 0)
    c = lax.broadcasted_iota(jnp.int32, (t, t), 1)
    causal = c <= r
    tril = jnp.where(causal, 1.0, 0.0).astype(F32)
    triu = jnp.where(r <= c, 1.0, 0.0).astype(F32)
    bc_all = jnp.dot(tril, gcol, preferred_element_type=F32, precision=lax.Precision.HIGHEST)
    br_all = jnp.dot(grow, triu, preferred_element_type=F32, precision=lax.Precision.HIGHEST)

    for h in range(n_heads):
        sl = slice(h * dh, (h + 1) * dh)
        qh, kh, vh = q_all[:, sl], k_all[:, sl], v_ref[:, sl]
        m_prev = m_ref[h][:, 0:1]
        b_c = bc_all[:, n_heads + h:n_heads + h + 1]
        li_c = gcol[:, h:h + 1]
        b_r = br_all[n_heads + h:n_heads + h + 1, :]
        li_r = grow[h:h + 1, :]
        log_d = jnp.where(causal, b_c - b_r + li_r, NEG)
        m_t = jnp.maximum(b_c + m_prev, jnp.max(log_d, axis=1, keepdims=True))
        s = _dot_nt(qh, kh) * jnp.exp(log_d - m_t)
        w_inter = jnp.exp(b_c + m_prev - m_t)
        ch = c_ref[h]
        num = _dot(s.astype(BF16), vh) + w_inter * _dot_nt(qh, ch.astype(BF16))
        nh = n_ref[h]
        den = jnp.sum(s, axis=1, keepdims=True) + w_inter * jnp.sum(qh.astype(F32) * nh, axis=1, keepdims=True)
        hout = num / jnp.maximum(jnp.abs(den), jnp.exp(-m_t))
        hn = hout * lax.rsqrt(jnp.mean(hout * hout, axis=1, keepdims=True) + EPS) * gn_ref[:, sl]
        y_ref[:, sl] = (_sigmoid(om_ref[:, sl].astype(F32)) * hn).astype(BF16)
        b_last = b_r[:, t - 1:t]
        log_w_r = b_last - b_r + li_r
        m_new = jnp.maximum(b_last + m_prev, jnp.max(log_w_r, axis=1, keepdims=True))
        w_c = jnp.exp(b_last - b_c + li_c - m_new)
        decay = jnp.exp(b_last + m_prev - m_new)
        wv = (w_c * vh.astype(F32)).astype(BF16)
        c_ref[h] = decay * ch + _dot_tn(wv, kh)
        n_ref[h] = decay * nh + jnp.sum(w_c * kh.astype(F32), axis=0, keepdims=True)
        m_ref[h] = jnp.broadcast_to(m_new, (1, LANE))


def _mlstm(qk, v, om, gcol, grow, conv_init, c0, n0, m0, conv_w, conv_b, g_norm, chunk, shared_init):
    bs, seq, dm2 = qk.shape
    dm = dm2 // 2
    n_heads, dh = c0.shape[1], c0.shape[2]
    ng = gcol.shape[2]
    nc = seq // chunk
    sidx = (lambda b: 0) if shared_init else (lambda b: b)
    row = lambda w: pl.BlockSpec((None, chunk, w), lambda b, i: (b, i, 0))
    return pl.pallas_call(
        functools.partial(_mlstm_kernel, n_heads=n_heads),
        out_shape=(jax.ShapeDtypeStruct((bs, seq, dm), BF16),
                   jax.ShapeDtypeStruct((bs, n_heads, dh, dh), F32),
                   jax.ShapeDtypeStruct((bs, n_heads, 1, dh), F32),
                   jax.ShapeDtypeStruct((bs, n_heads, 1, LANE), F32)),
        grid=(bs, nc),
        in_specs=[row(dm2), row(dm), row(dm), row(ng),
                  pl.BlockSpec((None, ng, chunk), lambda b, i: (b, 0, i)),
                  pl.BlockSpec((None, 8, dm2), lambda b, i: (sidx(b), 0, 0)),
                  pl.BlockSpec((None, n_heads, dh, dh), lambda b, i: (sidx(b), 0, 0, 0)),
                  pl.BlockSpec((None, n_heads, 1, dh), lambda b, i: (sidx(b), 0, 0, 0)),
                  pl.BlockSpec((None, n_heads, 1, LANE), lambda b, i: (sidx(b), 0, 0, 0)),
                  pl.BlockSpec((CONV_W, dm2), lambda b, i: (0, 0)),
                  pl.BlockSpec((1, dm2), lambda b, i: (0, 0)),
                  pl.BlockSpec((1, dm), lambda b, i: (0, 0))],
        out_specs=(row(dm),
                   pl.BlockSpec((None, n_heads, dh, dh), lambda b, i: (b, 0, 0, 0)),
                   pl.BlockSpec((None, n_heads, 1, dh), lambda b, i: (b, 0, 0, 0)),
                   pl.BlockSpec((None, n_heads, 1, LANE), lambda b, i: (b, 0, 0, 0))),
        scratch_shapes=[pltpu.VMEM((chunk + 8, dm2), F32)],
        compiler_params=_cparams("parallel", "arbitrary"),
        name="mlstm",
    )(qk, v, om, gcol, grow, conv_init, c0, n0, m0, conv_w, conv_b, g_norm)


def _layer_norm(z, g, b):
    mu = jnp.mean(z, axis=1, keepdims=True)
    zc = z - mu
    var = jnp.mean(zc * zc, axis=1, keepdims=True)
    return zc * lax.rsqrt(var + EPS) * g + b


def _merge_kernel(x_ref, ya_ref, ym_ref, ga_ref, gm_ref, wba_ref, wbm_ref, wout_ref, g_ref, b_ref,
                  wrh_ref, wrl_ref, h_ref, hrow_ref, s_ref, *, head_major):
    tm, d = x_ref.shape
    if head_major:
        ya = jnp.concatenate([ya_ref[h] for h in range(ya_ref.shape[0])], axis=1)
    else:
        ya = ya_ref[...]
    a = _dot(ya, wba_ref[...])
    m = _dot(ym_ref[...], wbm_ref[...])
    mix = ga_ref[...].astype(F32) * a + gm_ref[...].astype(F32) * m
    z = ALPHA * x_ref[...] + _dot(mix.astype(BF16), wout_ref[...])
    h = _layer_norm(z, g_ref[...], b_ref[...])
    h_ref[...] = h
    n_chunk = d // LANE
    for c in range(n_chunk):
        hrow_ref[pl.ds(c, tm, stride=n_chunk), :] = h[:, c * LANE:(c + 1) * LANE]
    hh = h.astype(BF16)
    hl = (h - hh.astype(F32)).astype(BF16)
    wrh = wrh_ref[...]
    s_ref[...] = _sigmoid(_dot(hh, wrh) + _dot(hl, wrh) + _dot(hh, wrl_ref[...]))


def _merge(x, ya, ym, ga, gm, wba, wbm, wout, g, b, wrh, wrl, tm, head_major):
    m, d = x.shape
    ne = wrh.shape[1]
    row = pl.BlockSpec((tm, d), lambda i: (i, 0))
    full = lambda a: pl.BlockSpec(a.shape, lambda i: (0,) * a.ndim)
    ya_spec = pl.BlockSpec((ya.shape[0], tm, LANE), lambda i: (0, i, 0)) if head_major else row
    return pl.pallas_call(
        functools.partial(_merge_kernel, head_major=head_major),
        out_shape=(jax.ShapeDtypeStruct((m, d), F32), jax.ShapeDtypeStruct((m * (d // LANE), LANE), F32),
                   jax.ShapeDtypeStruct((m, ne), F32)),
        grid=(m // tm,),
        in_specs=[row, ya_spec, row, row, row, full(wba), full(wbm), full(wout), full(g), full(b),
                  full(wrh), full(wrl)],
        out_specs=(row, pl.BlockSpec((tm * (d // LANE), LANE), lambda i: (i, 0)),
                   pl.BlockSpec((tm, ne), lambda i: (i, 0))),
        compiler_params=_cparams("parallel"),
        name="merge_ln_router",
    )(x, ya, ym, ga, gm, wba, wbm, wout, g, b, wrh, wrl)


def _route_kernel(s_ref, b_ref, idx_ref, rank_ref, gate_ref, cnt_ref, carry_ref, *, n_exp):
    i = pl.program_id(0)

    @pl.when(i == 0)
    def _init():
        carry_ref[...] = jnp.zeros(carry_ref.shape, F32)

    s = s_ref[...]
    tm = s.shape[0]
    ssel = s + b_ref[...]
    gsz = n_exp // N_GROUPS
    ninf = -jnp.inf
    lane_i = lax.broadcasted_iota(jnp.int32, (tm, n_exp), 1)
    lane = lane_i.astype(F32)
    lgrp = lane_i // gsz
    slot = lax.broadcasted_iota(jnp.int32, (tm, LANE), 1)

    grp = []
    for g in range(N_GROUPS):
        mg = jnp.where(lgrp == g, ssel, ninf)
        m1 = jnp.max(mg, axis=1, keepdims=True)
        i1 = jnp.min(jnp.where(mg == m1, lane, float(n_exp)), axis=1, keepdims=True)
        m2 = jnp.max(jnp.where(lane == i1, ninf, mg), axis=1, keepdims=True)
        grp.append(m1 + m2)
    enabled = jnp.zeros((tm, n_exp), F32)
    for g in range(N_GROUPS):
        ahead = jnp.zeros((tm, 1), F32)
        for j in range(N_GROUPS):
            if j != g:
                beats = (grp[j] > grp[g]) if j > g else (grp[j] >= grp[g])
                ahead = ahead + jnp.where(beats, 1.0, 0.0)
        keep = jnp.where(ahead < TOPK_GROUPS, 1.0, 0.0)
        enabled = jnp.where(lgrp == g, keep, enabled)
    cur = jnp.where(enabled > 0.5, ssel, ninf)

    sel = jnp.zeros((tm, n_exp), F32)
    idx_out = jnp.zeros((tm, LANE), F32)
    gate_out = jnp.zeros((tm, LANE), F32)
    gsum = jnp.zeros((tm, 1), F32)
    picks = []
    for k in range(TOP_K):
        m = jnp.max(cur, axis=1, keepdims=True)
        ik = jnp.min(jnp.where(cur == m, lane, float(n_exp)), axis=1, keepdims=True)
        pick = lane == ik
        gk = jnp.sum(jnp.where(pick, s, 0.0), axis=1, keepdims=True)
        cur = jnp.where(pick, ninf, cur)
        sel = jnp.where(pick, 1.0, sel)
        idx_out = jnp.where(slot == k, ik, idx_out)
        gate_out = jnp.where(slot == k, gk, gate_out)
        gsum = gsum + gk
        picks.append(ik)
    idx_ref[...] = idx_out
    gate_ref[...] = gate_out / gsum * ROUTE_SCALE

    r = lax.broadcasted_iota(jnp.int32, (tm, tm), 0)
    c = lax.broadcasted_iota(jnp.int32, (tm, tm), 1)
    ltri = jnp.where(c < r, 1.0, 0.0).astype(BF16)
    rank_dense = carry_ref[...] + _dot(ltri, sel.astype(BF16))
    rank_out = jnp.zeros((tm, LANE), F32)
    for k in range(TOP_K):
        rk = jnp.sum(jnp.where(lane == picks[k], rank_dense, 0.0), axis=1, keepdims=True)
        rank_out = jnp.where(slot == k, rk, rank_out)
    rank_ref[...] = rank_out
    total = carry_ref[...] + jnp.sum(sel, axis=0, keepdims=True)
    carry_ref[...] = total
    cnt_ref[...] = jnp.broadcast_to(total, cnt_ref.shape)


def _route(s, b_row, tm):
    t, n_exp = s.shape
    tok = jax.ShapeDtypeStruct((t, LANE), F32)
    tspec = pl.BlockSpec((tm, LANE), lambda i: (i, 0))
    return pl.pallas_call(
        functools.partial(_route_kernel, n_exp=n_exp),
        out_shape=(tok, tok, tok, jax.ShapeDtypeStruct((8, n_exp), F32)),
        grid=(t // tm,),
        in_specs=[pl.BlockSpec((tm, n_exp), lambda i: (i, 0)), pl.BlockSpec((1, n_exp), lambda i: (0, 0))],
        out_specs=(tspec, tspec, tspec, pl.BlockSpec((8, n_exp), lambda i: (0, 0))),
        scratch_shapes=[pltpu.VMEM((1, n_exp), F32)],
        compiler_params=_cparams("arbitrary"),
        name="route",
    )(s, b_row)


def _row_copy(src_ref, s0, dst_ref, d0, rows, sem):
    return pltpu.make_async_copy(src_ref.at[pl.ds(pl.multiple_of(s0, 8), rows)],
                                 dst_ref.at[pl.ds(pl.multiple_of(d0, 8), rows)], sem)


def _scatter_kernel(didx_ref, src_ref, dst_ref, sem, *, n_tok, fan, rows):
    def issue(t, carry):
        for u in range(fan):
            _row_copy(src_ref, t * rows, dst_ref, didx_ref[0, t * fan + u], rows, sem).start()
        return carry
    lax.fori_loop(0, n_tok, issue, 0)

    def drain(t, carry):
        for u in range(fan):
            _row_copy(src_ref, 0, dst_ref, 0, rows, sem).wait()
        return carry
    lax.fori_loop(0, n_tok, drain, 0)


def _scatter_rows(src, didx, n_dst_rows, n_tok, fan, rows):
    steps = src.shape[0] // (n_tok * rows)
    return pl.pallas_call(
        functools.partial(_scatter_kernel, n_tok=n_tok, fan=fan, rows=rows),
        out_shape=jax.ShapeDtypeStruct((n_dst_rows, src.shape[1]), src.dtype),
        grid=(steps,),
        in_specs=[pl.BlockSpec((None, 1, n_tok * fan), lambda i: (i, 0, 0), memory_space=pltpu.SMEM),
                  pl.BlockSpec((n_tok * rows, src.shape[1]), lambda i: (i, 0))],
        out_specs=pl.BlockSpec(memory_space=pl.ANY),
        scratch_shapes=[pltpu.SemaphoreType.DMA(())],
        compiler_params=_cparams("arbitrary"),
        name="scatter_rows",
    )(didx.reshape(steps, 1, n_tok * fan), src)


def _expert_kernel(win_ref, exp_ref, lo_ref, hi_ref, x_ref, w1_ref, w3_ref, w2_ref, y_ref,
                   w1b_ref, w3b_ref, w2b_ref, xb_ref, *, blk, n_chunk):
    s = pl.program_id(0)
    sp = jnp.maximum(s - 1, 0)
    new_win = jnp.logical_or(s == 0, win_ref[s] != win_ref[sp])
    new_exp = jnp.logical_or(s == 0, exp_ref[s] != exp_ref[sp])
    lo = lo_ref[s]
    hi = hi_ref[s]

    @pl.when(new_win)
    def _load_rows():
        for c in range(n_chunk):
            xb_ref[:, c * LANE:(c + 1) * LANE] = x_ref[pl.ds(c, blk, stride=n_chunk), :].astype(BF16)
        y_ref[...] = jnp.zeros(y_ref.shape, F32)

    @pl.when(new_exp)
    def _cast():
        w1b_ref[...] = w1_ref[...].astype(BF16)
        w3b_ref[...] = w3_ref[...].astype(BF16)
        w2b_ref[...] = w2_ref[...].astype(BF16)

    @pl.when(hi > lo)
    def _run():
        x = xb_ref[...]
        a = _dot(x, w1b_ref[...])
        g = (a * _sigmoid(a)) * _dot(x, w3b_ref[...])
        y = _dot(g.astype(BF16), w2b_ref[...])
        row = lax.broadcasted_iota(jnp.int32, (blk, LANE), 0)
        mine = jnp.logical_and(row >= lo, row < hi)
        for c in range(n_chunk):
            sl = pl.ds(c, blk, stride=n_chunk)
            y_ref[sl, :] = jnp.where(mine, y[:, c * LANE:(c + 1) * LANE], y_ref[sl, :])


def _experts(xrow, win, exp, lo, hi, w1, w3, w2, blk):
    d, de = w1.shape[1], w1.shape[2]
    n_chunk = d // LANE
    n_steps = win.shape[0]
    xspec = pl.BlockSpec((blk * n_chunk, LANE), lambda s, wn, ex, l, h: (wn[s], 0))
    grid_spec = pltpu.PrefetchScalarGridSpec(
        num_scalar_prefetch=4,
        grid=(n_steps,),
        in_specs=[xspec,
                  pl.BlockSpec((None, d, de), lambda s, wn, ex, l, h: (ex[s], 0, 0)),
                  pl.BlockSpec((None, d, de), lambda s, wn, ex, l, h: (ex[s], 0, 0)),
                  pl.BlockSpec((None, de, d), lambda s, wn, ex, l, h: (ex[s], 0, 0))],
        out_specs=xspec,
        scratch_shapes=[pltpu.VMEM((d, de), BF16), pltpu.VMEM((d, de), BF16), pltpu.VMEM((de, d), BF16),
                        pltpu.VMEM((blk, d), BF16)],
    )
    return pl.pallas_call(
        functools.partial(_expert_kernel, blk=blk, n_chunk=n_chunk),
        out_shape=jax.ShapeDtypeStruct(xrow.shape, F32),
        grid_spec=grid_spec,
        compiler_params=_cparams("arbitrary"),
        name="experts",
    )(win, exp, lo, hi, xrow, w1, w3, w2)


def _final_kernel(pos_ref, h_ref, gate_ref, yrow_ref, ws1_ref, ws3_ref, ws2_ref, g_ref, b_ref, o_ref,
                  yg_ref, sem, *, n_chunk):
    h = h_ref[...]
    tm = h.shape[0]

    def issue(t, carry):
        for k in range(TOP_K):
            j = t * TOP_K + k
            _row_copy(yrow_ref, pos_ref[0, j], yg_ref, j * n_chunk, n_chunk, sem).start()
        return carry
    lax.fori_loop(0, tm, issue, 0)

    def drain(t, carry):
        for k in range(TOP_K):
            _row_copy(yrow_ref, 0, yg_ref, 0, n_chunk, sem).wait()
        return carry
    lax.fori_loop(0, tm, drain, 0)

    x = h.astype(BF16)
    a = _dot(x, ws1_ref[...])
    gg = (a * _sigmoid(a)) * _dot(x, ws3_ref[...])
    shared = _dot(gg.astype(BF16), ws2_ref[...])
    gate = gate_ref[...]
    gk = [jnp.broadcast_to(gate[:, k:k + 1], (tm, LANE)) for k in range(TOP_K)]
    stride = TOP_K * n_chunk
    chunks = []
    for c in range(n_chunk):
        acc = gk[0] * yg_ref[pl.ds(c, tm, stride=stride), :]
        for k in range(1, TOP_K):
            acc = acc + gk[k] * yg_ref[pl.ds(k * n_chunk + c, tm, stride=stride), :]
        chunks.append(acc)
    f = jnp.concatenate(chunks, axis=1) + shared
    o_ref[...] = _layer_norm(ALPHA * h + f, g_ref[...], b_ref[...])


def _final(h, yrow, pos, gate, ws1, ws3, ws2, g, b, tm, blk_off):
    m, d = h.shape
    n_chunk = d // LANE
    row = pl.BlockSpec((tm, d), lambda i: (i, 0))
    full = lambda a: pl.BlockSpec(a.shape, lambda i: (0,) * a.ndim)
    return pl.pallas_call(
        functools.partial(_final_kernel, n_chunk=n_chunk),
        out_shape=jax.ShapeDtypeStruct((m, d), F32),
        grid=(m // tm,),
        in_specs=[pl.BlockSpec((None, 1, tm * TOP_K), lambda i: (i + blk_off, 0, 0), memory_space=pltpu.SMEM),
                  row, pl.BlockSpec((tm, LANE), lambda i: (i + blk_off, 0)),
                  pl.BlockSpec(memory_space=pl.ANY),
                  full(ws1), full(ws3), full(ws2), full(g), full(b)],
        out_specs=row,
        scratch_shapes=[pltpu.VMEM((tm * TOP_K * n_chunk, LANE), F32), pltpu.SemaphoreType.DMA(())],
        compiler_params=_cparams("arbitrary"),
        name="shared_ln",
    )(pos.reshape(-1, 1, tm * TOP_K), h, gate, yrow, ws1, ws3, ws2, g, b)


def _rope_tables(pos):
    half = ROPE_DIM // 2
    inv = ROPE_THETA ** (-jnp.arange(half, dtype=F32) * 2.0 / ROPE_DIM)
    ang = pos.astype(F32)[:, None] * inv
    cos8, sin8 = jnp.cos(ang), jnp.sin(ang)
    n = pos.shape[0]
    cos = jnp.concatenate([cos8, cos8, jnp.ones((n, D_QK - ROPE_DIM), F32)], axis=1)
    sin = jnp.concatenate([-sin8, sin8, jnp.zeros((n, D_QK - ROPE_DIM), F32)], axis=1)
    return jnp.tile(cos, (1, LANE // D_QK)), jnp.tile(sin, (1, LANE // D_QK))


def _expert_steps(starts, ends, n_win, blk):
    n_exp = starts.shape[0]
    first_w = starts // blk
    n_w = jnp.where(ends > starts, (ends + blk - 1) // blk - first_w, 0)
    step_end = jnp.cumsum(n_w)
    step_start = step_end - n_w
    total = step_end[-1]
    s = jnp.arange(n_win + n_exp, dtype=jnp.int32)
    e = jnp.minimum(jnp.searchsorted(step_end, s, side='right'), n_exp - 1).astype(jnp.int32)
    w = first_w[e] + (s - step_start[e])
    lo = jnp.maximum(starts[e], w * blk) - w * blk
    hi = jnp.minimum(ends[e], (w + 1) * blk) - w * blk
    valid = s < total
    last = jnp.maximum(total - 1, 0)
    e = jnp.where(valid, e, e[last])
    w = jnp.where(valid, w, w[last])
    lo = jnp.where(valid, lo, 0)
    hi = jnp.where(valid, hi, 0)
    return w.astype(jnp.int32), e, lo.astype(jnp.int32), hi.astype(jnp.int32)


def _row_tile(m, pref):
    t = pref
    while m % t:
        t //= 2
    return t


def kernel(x_prompt, x_sample, cache_k, cache_v, page_table, state_C, state_n, state_m, state_conv, meta_tokens, w_in, b_in, lam_q1, lam_k1, lam_q2, lam_k2, g_subln, conv_w, conv_b, g_mnorm, w_ba, w_bm, w_out, ln1_g, ln1_b, w_router, b_router, w1, w3, w2, ws1, ws3, ws2, ln2_g, ln2_b):
    B, SEQ, D = x_prompt.shape
    BD, S, _ = x_sample.shape
    NHA, DVA = cache_k.shape[3], cache_v.shape[4]
    NHM, DHM = state_C.shape[2], state_C.shape[3]
    n_exp = w_router.shape[2]
    n_pages = page_table.shape[1]
    page = cache_k.shape[2]
    l = 0
    lam_init = 0.8 - 0.6 * math.exp(-0.3 * l)
    lam = (jnp.exp(jnp.sum(lam_q1[l] * lam_k1[l])) - jnp.exp(jnp.sum(lam_q2[l] * lam_k2[l])) + lam_init)
    lam_row = jnp.broadcast_to(lam.astype(F32), (1, LANE))
    g_row = g_subln[l].astype(F32).reshape(1, LANE)

    w = w_in[l]
    bias = b_in[l].astype(F32)
    o_mix = 3 * D
    o_if = o_mix + 4 * D
    o_g = o_if + 2 * NHM
    w_qkv = w[:, :o_mix].astype(BF16)
    w_mix = w[:, o_mix:o_if].astype(BF16)
    w_g = w[:, o_g:].astype(BF16)
    w_if = w[:, o_if:o_g]
    b_qkv = bias[None, :o_mix]
    b_mix = bias[None, o_mix:o_if]
    b_g = bias[None, o_g:]
    b_if = bias[o_if:o_g]

    MP = B * SEQ
    xp = x_prompt.reshape(MP, D)
    xs = x_sample.reshape(BD * S, D)
    xm = jnp.zeros((PAD_CHUNK, D), F32).at[:N_META].set(meta_tokens.astype(F32))
    tmp = _row_tile(SEQ, 512)
    tms = _row_tile(BD * S, 256)

    cos_p, sin_p = _rope_tables(N_META + jnp.arange(SEQ))
    cos_s, sin_s = _rope_tables(jnp.tile(n_pages * page + jnp.arange(S), BD))
    cos_m, sin_m = _rope_tables(jnp.arange(PAD_CHUNK))

    xp_bf, xs_bf, xm_bf = xp.astype(BF16), xs.astype(BF16), xm.astype(BF16)
    qlo_p, qhi_p, kf_p, kb_p, vf_p, vb_p = _proj_qkv(xp_bf, w_qkv, b_qkv, cos_p, sin_p, tmp, NHA)
    qlo_s, qhi_s, kf_s, _, vf_s, _ = _proj_qkv(xs_bf, w_qkv, b_qkv, cos_s, sin_s, tms, NHA)
    _, _, kf_m, kb_m, vf_m, vb_m = _proj_qkv(xm_bf, w_qkv, b_qkv, cos_m, sin_m, PAD_CHUNK, NHA)

    qk_p, vm_p, om_p = _proj_mix(xp_bf, w_mix, b_mix, tmp)
    qk_s, vm_s, om_s = _proj_mix(xs_bf, w_mix, b_mix, tms)
    qk_m, vm_m, om_m = _proj_mix(xm_bf, w_mix, b_mix, PAD_CHUNK)

    ga_p, gm_p = _proj_gates(xp_bf, w_g, b_g, tmp)
    ga_s, gm_s = _proj_gates(xs_bf, w_g, b_g, tms)

    if_p = _proj_if(xp, w_if, b_if, tmp, NHM)[:, :2 * NHM]
    if_s = _proj_if(xs, w_if, b_if, tms, NHM)[:, :2 * NHM]
    if_m = _proj_if(xm, w_if, b_if, PAD_CHUNK, NHM)[:, :2 * NHM]

    ya_p = _prompt_attention(qlo_p, qhi_p, kb_p, vb_p, kb_m, vb_m, lam_row, g_row, B, lam_init,
                             _row_tile(SEQ, ATTN_TILE))
    q_s = jnp.transpose(qlo_s + qhi_s, (1, 0, 2)).reshape(BD, S, D).astype(F32)
    kn = jnp.zeros((BD, LANE, D), BF16).at[:, :S].set(kf_s.reshape(BD, S, D).astype(BF16))
    vn = jnp.zeros((BD, LANE, D), BF16).at[:, :S].set(vf_s.reshape(BD, S, D).astype(BF16))
    n_pool = cache_k.shape[1]
    ya_s = _paged_attention(q_s, kn, vn, cache_k.reshape(-1, page * NHA, DVA), cache_v.reshape(-1, page * NHA, DVA),
                            page_table + l * n_pool, lam_row, g_row, NHA, lam_init,
                            math.gcd(PAGES_PER_STEP, n_pages))

    cw = conv_w[l].astype(F32)
    cb = conv_b[l].astype(F32)[None]
    gn = g_mnorm[l].astype(F32)[None]
    pad_gate = jnp.concatenate([jnp.full((NHM,), NEG, F32), jnp.zeros((NHM,), F32)])

    def gate_layouts(g, bs, real, total):
        g = g.reshape(bs, -1, 2 * NHM)[:, :real]
        if total > real:
            g = jnp.concatenate([g, jnp.broadcast_to(pad_gate, (bs, total - real, 2 * NHM))], axis=1)
        return g, jnp.transpose(g, (0, 2, 1))

    def pad_rows(a, bs, real, total):
        a = a.reshape(bs, -1, a.shape[-1])[:, :real]
        return jnp.pad(a, ((0, 0), (0, total - real), (0, 0)))

    gcol_m, grow_m = gate_layouts(if_m, 1, N_META, PAD_CHUNK)
    zC = jnp.zeros((1, NHM, DHM, DHM), F32)
    zn = jnp.zeros((1, NHM, 1, DHM), F32)
    zm = jnp.zeros((1, NHM, 1, LANE), F32)
    _, c_m, n_m, m_m = _mlstm(pad_rows(qk_m, 1, N_META, PAD_CHUNK), pad_rows(vm_m, 1, N_META, PAD_CHUNK),
                              pad_rows(om_m, 1, N_META, PAD_CHUNK), gcol_m, grow_m,
                              jnp.zeros((1, 8, 2 * D), F32), zC, zn, zm, cw, cb, gn, PAD_CHUNK, True)
    hist_m = jnp.zeros((1, 8, 2 * D), F32).at[0, 8 - (CONV_W - 1):].set(qk_m[N_META - (CONV_W - 1):N_META])
    gcol_p, grow_p = gate_layouts(if_p, B, SEQ, SEQ)
    ym_p, c_p, n_p, m_p = _mlstm(qk_p.reshape(B, SEQ, 2 * D), vm_p.reshape(B, SEQ, D), om_p.reshape(B, SEQ, D),
                                 gcol_p, grow_p, hist_m, c_m, n_m, m_m, cw, cb, gn,
                                 _row_tile(SEQ, MLSTM_CHUNK), True)
    gcol_s, grow_s = gate_layouts(if_s, BD, S, PAD_CHUNK)
    hist_s = jnp.zeros((BD, 8, 2 * D), F32).at[:, 8 - (CONV_W - 1):].set(state_conv[l].astype(F32))
    ym_s, c_s, n_s, m_s = _mlstm(pad_rows(qk_s, BD, S, PAD_CHUNK), pad_rows(vm_s, BD, S, PAD_CHUNK),
                                 pad_rows(om_s, BD, S, PAD_CHUNK), gcol_s, grow_s, hist_s,
                                 state_C[l].astype(F32), state_n[l].astype(F32)[:, :, None, :],
                                 jnp.broadcast_to(state_m[l].astype(F32)[:, :, None, None], (BD, NHM, 1, LANE)),
                                 cw, cb, gn, PAD_CHUNK, False)
    ym_s = ym_s[:, :S].reshape(BD * S, D)

    wba, wbm, wout = w_ba[l].astype(BF16), w_bm[l].astype(BF16), w_out[l].astype(BF16)
    g1, b1 = ln1_g[l].astype(F32)[None], ln1_b[l].astype(F32)[None]
    wr = w_router[l].astype(F32)
    wrh = wr.astype(BF16)
    wrl = (wr - wrh.astype(F32)).astype(BF16)
    h_p, hrow_p, s_p = _merge(xp, ya_p, ym_p.reshape(MP, D), ga_p, gm_p, wba, wbm, wout, g1, b1, wrh, wrl,
                              _row_tile(MP, 256), True)
    h_s, hrow_s, s_s = _merge(xs, ya_s.reshape(BD * S, D).astype(BF16), ym_s, ga_s, gm_s, wba, wbm, wout,
                              g1, b1, wrh, wrl, tms, False)

    MT = MP + BD * S
    n_chunk = D // LANE
    tk = MT * TOP_K
    tt = _row_tile(math.gcd(MP, BD * S), 256)
    hrow = jnp.concatenate([hrow_p, hrow_s], axis=0)
    s_all = jnp.concatenate([s_p, s_s], axis=0)
    idx_f, rank_f, gate, cnt = _route(s_all, b_router[l].astype(F32)[None], tt)
    idx = idx_f[:, :TOP_K].astype(jnp.int32)
    counts = cnt[0].astype(jnp.int32)
    ends = jnp.cumsum(counts)
    starts = ends - counts
    start_of = jnp.sum(jnp.where(idx[..., None] == jnp.arange(n_exp, dtype=jnp.int32), starts, 0), axis=-1)
    pos = ((start_of + rank_f[:, :TOP_K].astype(jnp.int32)) * n_chunk).reshape(tk)
    n_win = -(-tk // MOE_BLOCK)
    xrow = _scatter_rows(hrow, pos, n_win * MOE_BLOCK * n_chunk, tt, TOP_K, n_chunk)
    win, exp, lo, hi = _expert_steps(starts, ends, n_win, MOE_BLOCK)
    yrow = _experts(xrow, win, exp, lo, hi, w1[l], w3[l], w2[l], MOE_BLOCK)

    ws1b, ws3b, ws2b = ws1[l].astype(BF16), ws3[l].astype(BF16), ws2[l].astype(BF16)
    g2, b2 = ln2_g[l].astype(F32)[None], ln2_b[l].astype(F32)[None]
    y_p = _final(h_p, yrow, pos, gate, ws1b, ws3b, ws2b, g2, b2, tt, 0)
    y_s = _final(h_s, yrow, pos, gate, ws1b, ws3b, ws2b, g2, b2, tt, MP // tt)

    def with_meta(meta_rows, real):
        meta = jnp.broadcast_to(meta_rows[None, :N_META], (B, N_META, D))
        return jnp.concatenate([meta, real.reshape(B, SEQ, D)], axis=1).reshape(1, B, N_META + SEQ, NHA, DVA)

    k_prompt = with_meta(kf_m, kf_p)
    v_prompt = with_meta(vf_m, vf_p)
    conv_prompt = qk_p.reshape(B, SEQ, 2 * D)[:, SEQ - (CONV_W - 1):][None]
    conv_sample = qk_s.reshape(BD, S, 2 * D)[:, S - (CONV_W - 1):][None]
    return (y_p.reshape(B, SEQ, D), y_s.reshape(BD, S, D),
            k_prompt, v_prompt,
            c_p[None], n_p[:, :, 0][None], m_p[:, :, 0, 0][None], conv_prompt,
            kf_s.reshape(1, BD, S, NHA, DVA), vf_s.reshape(1, BD, S, NHA, DVA),
            c_s[None], n_s[:, :, 0][None], m_s[:, :, 0, 0][None], conv_sample)
```

```python
import functools
import math

import jax
import jax.numpy as jnp
from jax import lax
from jax.experimental import pallas as pl
from jax.experimental.pallas import tpu as pltpu

F32 = jnp.float32
BF16 = jnp.bfloat16

DEPTH = 1
N_META = 16
D_QK = 64
ROPE_DIM = D_QK // 4
ROPE_THETA = 500000.0
CONV_W = 4
TOP_K = 8
N_GROUPS = 8
TOPK_GROUPS = 4
ROUTE_SCALE = 2.5
ALPHA = (2 * DEPTH) ** 0.25
EPS = 1e-5
NEG = -1e30

LANE = 128
VMEM_LIMIT = 52 * 1024 * 1024

ATTN_TILE = 512
MLSTM_CHUNK = 256
PAD_CHUNK = 128
PAGES_PER_STEP = 8
MOE_BLOCK = 256


def _cparams(*sem):
    return pltpu.CompilerParams(dimension_semantics=sem, vmem_limit_bytes=VMEM_LIMIT)


def _dot(a, b):
    return jnp.dot(a, b, preferred_element_type=F32)


def _dot_nt(a, b):
    return lax.dot_general(a, b, (((1,), (1,)), ((), ())), preferred_element_type=F32)


def _dot_tn(a, b):
    return lax.dot_general(a, b, (((0,), (0,)), ((), ())), preferred_element_type=F32)


def _split3(a):
    a1 = a.astype(BF16)
    r1 = a - a1.astype(F32)
    a2 = r1.astype(BF16)
    a3 = (r1 - a2.astype(F32)).astype(BF16)
    return a1, a2, a3


def _sigmoid(x):
    return 1.0 / (1.0 + jnp.exp(-x))


def _qkv_kernel(x_ref, w_ref, b_ref, cos_ref, sin_ref,
                qlo_ref, qhi_ref, kf_ref, kb_ref, vf_ref, vb_ref, *, n_heads, d_model):
    x = x_ref[...]
    cos = cos_ref[...]
    sin = sin_ref[...]
    lane = lax.broadcasted_iota(jnp.int32, cos.shape, 1)
    first = (lane % D_QK) < (ROPE_DIM // 2)
    lo_half = lane < D_QK
    half = ROPE_DIM // 2

    def rope(s):
        up = pltpu.roll(s, LANE - half, axis=1)
        dn = pltpu.roll(s, half, axis=1)
        return s * cos + jnp.where(first, up, dn) * sin

    zq = _dot(x, w_ref[:, 0:d_model]) + b_ref[:, 0:d_model]
    for h in range(n_heads):
        q = rope(zq[:, h * LANE:(h + 1) * LANE]) * (D_QK ** -0.5)
        qlo_ref[h] = jnp.where(lo_half, q, 0.0).astype(BF16)
        qhi_ref[h] = jnp.where(lo_half, 0.0, q).astype(BF16)
    zk = _dot(x, w_ref[:, d_model:2 * d_model]) + b_ref[:, d_model:2 * d_model]
    for h in range(n_heads):
        k = rope(zk[:, h * LANE:(h + 1) * LANE])
        kf_ref[:, h * LANE:(h + 1) * LANE] = k
        kb_ref[h] = k.astype(BF16)
    zv = _dot(x, w_ref[:, 2 * d_model:3 * d_model]) + b_ref[:, 2 * d_model:3 * d_model]
    vf_ref[...] = zv
    for h in range(n_heads):
        vb_ref[h] = zv[:, h * LANE:(h + 1) * LANE].astype(BF16)


def _proj_qkv(x_bf, w_bf, bias, cos, sin, tm, n_heads):
    m, d = x_bf.shape
    n_pos = cos.shape[0] // tm
    hm = jax.ShapeDtypeStruct((n_heads, m, LANE), BF16)
    fm = jax.ShapeDtypeStruct((m, d), F32)
    hspec = pl.BlockSpec((n_heads, tm, LANE), lambda i: (0, i, 0))
    fspec = pl.BlockSpec((tm, d), lambda i: (i, 0))
    pspec = pl.BlockSpec((tm, LANE), lambda i: (i % n_pos, 0))
    return pl.pallas_call(
        functools.partial(_qkv_kernel, n_heads=n_heads, d_model=d),
        out_shape=(hm, hm, fm, hm, fm, hm),
        grid=(m // tm,),
        in_specs=[fspec, pl.BlockSpec((d, 3 * d), lambda i: (0, 0)),
                  pl.BlockSpec((1, 3 * d), lambda i: (0, 0)), pspec, pspec],
        out_specs=(hspec, hspec, fspec, hspec, fspec, hspec),
        compiler_params=_cparams("parallel"),
        name="proj_qkv",
    )(x_bf, w_bf, bias, cos, sin)


def _mix_proj_kernel(x_ref, w_ref, b_ref, qk_ref, v_ref, om_ref, *, d_model):
    x = x_ref[...]
    d = d_model
    qk_ref[...] = _dot(x, w_ref[:, 0:2 * d]) + b_ref[:, 0:2 * d]
    v_ref[...] = (_dot(x, w_ref[:, 2 * d:3 * d]) + b_ref[:, 2 * d:3 * d]).astype(BF16)
    om_ref[...] = (_dot(x, w_ref[:, 3 * d:4 * d]) + b_ref[:, 3 * d:4 * d]).astype(BF16)


def _proj_mix(x_bf, w_bf, bias, tm):
    m, d = x_bf.shape
    return pl.pallas_call(
        functools.partial(_mix_proj_kernel, d_model=d),
        out_shape=(jax.ShapeDtypeStruct((m, 2 * d), F32), jax.ShapeDtypeStruct((m, d), BF16),
                   jax.ShapeDtypeStruct((m, d), BF16)),
        grid=(m // tm,),
        in_specs=[pl.BlockSpec((tm, d), lambda i: (i, 0)),
                  pl.BlockSpec((d, 4 * d), lambda i: (0, 0)),
                  pl.BlockSpec((1, 4 * d), lambda i: (0, 0))],
        out_specs=(pl.BlockSpec((tm, 2 * d), lambda i: (i, 0)), pl.BlockSpec((tm, d), lambda i: (i, 0)),
                   pl.BlockSpec((tm, d), lambda i: (i, 0))),
        compiler_params=_cparams("parallel"),
        name="proj_mix",
    )(x_bf, w_bf, bias)


def _gate_proj_kernel(x_ref, w_ref, b_ref, ga_ref, gm_ref, *, d_model):
    x = x_ref[...]
    d = d_model
    ga_ref[...] = _sigmoid(_dot(x, w_ref[:, 0:d]) + b_ref[:, 0:d]).astype(BF16)
    gm_ref[...] = _sigmoid(_dot(x, w_ref[:, d:2 * d]) + b_ref[:, d:2 * d]).astype(BF16)


def _proj_gates(x_bf, w_bf, bias, tm):
    m, d = x_bf.shape
    return pl.pallas_call(
        functools.partial(_gate_proj_kernel, d_model=d),
        out_shape=(jax.ShapeDtypeStruct((m, d), BF16), jax.ShapeDtypeStruct((m, d), BF16)),
        grid=(m // tm,),
        in_specs=[pl.BlockSpec((tm, d), lambda i: (i, 0)),
                  pl.BlockSpec((d, 2 * d), lambda i: (0, 0)),
                  pl.BlockSpec((1, 2 * d), lambda i: (0, 0))],
        out_specs=(pl.BlockSpec((tm, d), lambda i: (i, 0)), pl.BlockSpec((tm, d), lambda i: (i, 0))),
        compiler_params=_cparams("parallel"),
        name="proj_gates",
    )(x_bf, w_bf, bias)


def _if_proj_kernel(x_ref, wh_ref, wl_ref, b_ref, o_ref, *, n_heads):
    x = x_ref[...]
    xh = x.astype(BF16)
    xl = (x - xh.astype(F32)).astype(BF16)
    wh = wh_ref[...]
    z = _dot(xh, wh) + _dot(xl, wh) + _dot(xh, wl_ref[...]) + b_ref[...]
    lane = lax.broadcasted_iota(jnp.int32, z.shape, 1)
    logsig = jnp.minimum(z, 0.0) - jnp.log(1.0 + jnp.exp(-jnp.abs(z)))
    o_ref[...] = jnp.where(lane < n_heads, z, logsig)


def _proj_if(x_f32, w_if, b_if, tm, n_heads):
    m, d = x_f32.shape
    wpad = jnp.zeros((d, LANE), F32).at[:, :2 * n_heads].set(w_if)
    bpad = jnp.zeros((1, LANE), F32).at[0, :2 * n_heads].set(b_if)
    wh = wpad.astype(BF16)
    wl = (wpad - wh.astype(F32)).astype(BF16)
    return pl.pallas_call(
        functools.partial(_if_proj_kernel, n_heads=n_heads),
        out_shape=jax.ShapeDtypeStruct((m, LANE), F32),
        grid=(m // tm,),
        in_specs=[pl.BlockSpec((tm, d), lambda i: (i, 0)),
                  pl.BlockSpec((d, LANE), lambda i: (0, 0)),
                  pl.BlockSpec((d, LANE), lambda i: (0, 0)),
                  pl.BlockSpec((1, LANE), lambda i: (0, 0))],
        out_specs=pl.BlockSpec((tm, LANE), lambda i: (i, 0)),
        compiler_params=_cparams("parallel"),
        name="proj_if",
    )(x_f32, wh, wl, bpad)


def _flash_update(idx, q, kvs, m_ref, l_ref, acc_ref, first):
    ss = []
    for k, _, mask in kvs:
        s = _dot_nt(q, k)
        ss.append(s if mask is None else jnp.where(mask, s, NEG))
    m_cur = functools.reduce(jnp.maximum, [jnp.max(s, axis=1, keepdims=True) for s in ss])
    if first:
        m_new = jnp.broadcast_to(m_cur, (q.shape[0], LANE))
    else:
        m_prev = m_ref[idx]
        m_new = jnp.maximum(m_prev, m_cur)
    lsum, pv = None, None
    for s, (_, v, _) in zip(ss, kvs):
        p = jnp.exp(s - jnp.tile(m_new, (1, s.shape[1] // LANE)))
        ls = jnp.sum(p, axis=1, keepdims=True)
        d = _dot(p.astype(BF16), v)
        lsum = ls if lsum is None else lsum + ls
        pv = d if pv is None else pv + d
    if first:
        l_ref[idx] = jnp.broadcast_to(lsum, (q.shape[0], LANE))
        acc_ref[idx] = pv
    else:
        alpha = jnp.exp(m_prev - m_new)
        l_ref[idx] = alpha * l_ref[idx] + lsum
        acc_ref[idx] = alpha * acc_ref[idx] + pv
    m_ref[idx] = m_new


def _attn_kernel(qi_tab, kj_tab, qlo_ref, qhi_ref, k_ref, v_ref, km_ref, vm_ref, lam_ref, g_ref,
                 o_ref, m_ref, l_ref, acc_ref, *, n_heads, lam_init):
    s_idx = pl.program_id(1)
    qi = qi_tab[s_idx]
    kj = kj_tab[s_idx]
    tq = qlo_ref.shape[1]
    tk = k_ref.shape[1]

    def head_loop(kvs_of, first, finish=None):
        def body(h, c):
            kvs = kvs_of(h)
            _flash_update(2 * h, qlo_ref[h], kvs, m_ref, l_ref, acc_ref, first)
            _flash_update(2 * h + 1, qhi_ref[h], kvs, m_ref, l_ref, acc_ref, first)
            if finish is not None:
                finish(h)
            return c
        lax.fori_loop(0, n_heads, body, 0)

    @pl.when(jnp.logical_and(kj == 0, qi > 0))
    def _first():
        head_loop(lambda h: [(k_ref[h], v_ref[h], None)], True)

    @pl.when(jnp.logical_and(kj > 0, kj < qi))
    def _full():
        head_loop(lambda h: [(k_ref[h], v_ref[h], None)], False)

    @pl.when(kj == qi)
    def _diag():
        @pl.when(qi == 0)
        def _init():
            m_ref[...] = jnp.full(m_ref.shape, NEG, F32)
            l_ref[...] = jnp.zeros(l_ref.shape, F32)
            acc_ref[...] = jnp.zeros(acc_ref.shape, F32)

        row = lax.broadcasted_iota(jnp.int32, (tq, tk), 0)
        col = lax.broadcasted_iota(jnp.int32, (tq, tk), 1)
        causal = col <= row
        is_meta = lax.broadcasted_iota(jnp.int32, (tq, km_ref.shape[1]), 1) < N_META
        lam = lam_ref[...]
        g = g_ref[...] * (1.0 - lam_init)

        def finish(h):
            o = acc_ref[2 * h] / l_ref[2 * h] - lam * (acc_ref[2 * h + 1] / l_ref[2 * h + 1])
            o = o * lax.rsqrt(jnp.mean(o * o, axis=1, keepdims=True) + EPS) * g
            o_ref[h] = o.astype(BF16)

        head_loop(lambda h: [(k_ref[h], v_ref[h], causal), (km_ref[h], vm_ref[h], is_meta)], False, finish)


def _prompt_attention(qlo, qhi, kb, vb, kmeta, vmeta, lam_row, g_row, batch, lam_init, tile):
    n_heads, m, _ = qlo.shape
    seq = m // batch
    nq = seq // tile
    qi_list, kj_list = [], []
    for qi in range(nq):
        for kj in range(qi + 1):
            qi_list.append(qi)
            kj_list.append(kj)
    qi_tab = jnp.asarray(qi_list, jnp.int32)
    kj_tab = jnp.asarray(kj_list, jnp.int32)
    n_steps = len(qi_list)
    tmeta = kmeta.shape[1]
    qspec = pl.BlockSpec((n_heads, tile, LANE), lambda b, s, qt, kt: (0, b * nq + qt[s], 0))
    kspec = pl.BlockSpec((n_heads, tile, LANE), lambda b, s, qt, kt: (0, b * nq + kt[s], 0))
    mspec = pl.BlockSpec((n_heads, tmeta, LANE), lambda b, s, qt, kt: (0, 0, 0))
    rspec = pl.BlockSpec((1, LANE), lambda b, s, qt, kt: (0, 0))
    grid_spec = pltpu.PrefetchScalarGridSpec(
        num_scalar_prefetch=2,
        grid=(batch, n_steps),
        in_specs=[qspec, qspec, kspec, kspec, mspec, mspec, rspec, rspec],
        out_specs=qspec,
        scratch_shapes=[pltpu.VMEM((2 * n_heads, tile, LANE), F32),
                        pltpu.VMEM((2 * n_heads, tile, LANE), F32),
                        pltpu.VMEM((2 * n_heads, tile, LANE), F32)],
    )
    return pl.pallas_call(
        functools.partial(_attn_kernel, n_heads=n_heads, lam_init=lam_init),
        out_shape=jax.ShapeDtypeStruct((n_heads, m, LANE), BF16),
        grid_spec=grid_spec,
        compiler_params=_cparams("parallel", "arbitrary"),
        name="prompt_attention",
    )(qi_tab, kj_tab, qlo, qhi, kb, vb, kmeta, vmeta, lam_row, g_row)


def _paged_kernel(pt_ref, q_ref, *refs, n_heads, n_tok, pages, page, lam_init):
    k_refs = refs[:pages]
    v_refs = refs[pages:2 * pages]
    kn_ref, vn_ref, lam_ref, g_ref, o_ref, qbd_ref, kb_ref, vb_ref, m_ref, l_ref, acc_ref = refs[2 * pages:]
    j = pl.program_id(1)
    rows = 2 * n_heads * n_tok
    d = q_ref.shape[1]

    @pl.when(j == 0)
    def _init():
        q = q_ref[...]
        qt = jnp.tile(q, (rows // n_tok, 1))
        r = lax.broadcasted_iota(jnp.int32, (rows, d), 0)
        c = lax.broadcasted_iota(jnp.int32, (rows, d), 1)
        qbd_ref[...] = jnp.where((c // D_QK) == (r // n_tok), qt, 0.0).astype(BF16)
        m_ref[...] = jnp.full(m_ref.shape, NEG, F32)
        l_ref[...] = jnp.zeros(l_ref.shape, F32)
        acc_ref[...] = jnp.zeros(acc_ref.shape, F32)

    def update(k, v, mask):
        s = _dot_nt(qbd_ref[...], k)
        if mask is not None:
            s = jnp.where(mask, s, NEG)
        m_prev = m_ref[...]
        m_new = jnp.maximum(m_prev, jnp.max(s, axis=1, keepdims=True))
        alpha = jnp.exp(m_prev - m_new)
        p = jnp.exp(s - jnp.tile(m_new, (1, s.shape[1] // LANE)))
        l_ref[...] = alpha * l_ref[...] + jnp.sum(p, axis=1, keepdims=True)
        acc_ref[...] = jnp.tile(alpha, (1, d // LANE)) * acc_ref[...] + _dot(p.astype(BF16), v)
        m_ref[...] = m_new

    for p in range(pages):
        for h in range(n_heads):
            kb_ref[p * page:(p + 1) * page, h * LANE:(h + 1) * LANE] = (
                k_refs[p][pl.ds(h, page, stride=n_heads), :].astype(BF16))
            vb_ref[p * page:(p + 1) * page, h * LANE:(h + 1) * LANE] = (
                v_refs[p][pl.ds(h, page, stride=n_heads), :].astype(BF16))
    update(kb_ref[...], vb_ref[...], None)

    @pl.when(j == pl.num_programs(1) - 1)
    def _final():
        r = lax.broadcasted_iota(jnp.int32, (rows, kn_ref.shape[0]), 0)
        c = lax.broadcasted_iota(jnp.int32, (rows, kn_ref.shape[0]), 1)
        update(kn_ref[...], vn_ref[...], c <= (r % n_tok))
        o = acc_ref[...] / jnp.tile(l_ref[...], (1, d // LANE))
        lam = lam_ref[...]
        g = g_ref[...] * (1.0 - lam_init)
        for h in range(n_heads):
            r0 = 2 * h * n_tok
            o1 = o[r0:r0 + n_tok, h * LANE:(h + 1) * LANE]
            o2 = o[r0 + n_tok:r0 + 2 * n_tok, h * LANE:(h + 1) * LANE]
            oh = o1 - lam * o2
            oh = oh * lax.rsqrt(jnp.mean(oh * oh, axis=1, keepdims=True) + EPS) * g
            o_ref[:, h * LANE:(h + 1) * LANE] = oh


def _paged_attention(q, knew, vnew, cache_k, cache_v, page_table, lam_row, g_row, n_heads, lam_init, pages):
    bd, n_tok, d = q.shape
    page = cache_k.shape[1] // n_heads
    n_pages = page_table.shape[1]
    rows = 2 * n_heads * n_tok

    def page_spec(p):
        return pl.BlockSpec((None, page * n_heads, LANE),
                            lambda b, j, pt: (pt[b * n_pages + j * pages + p], 0, 0))

    bspec = pl.BlockSpec((None, n_tok, d), lambda b, j, pt: (b, 0, 0))
    nspec = pl.BlockSpec((None, knew.shape[1], d), lambda b, j, pt: (b, 0, 0))
    rspec = pl.BlockSpec((1, LANE), lambda b, j, pt: (0, 0))
    grid_spec = pltpu.PrefetchScalarGridSpec(
        num_scalar_prefetch=1,
        grid=(bd, n_pages // pages),
        in_specs=[bspec] + [page_spec(p) for p in range(pages)] * 2 + [nspec, nspec, rspec, rspec],
        out_specs=bspec,
        scratch_shapes=[pltpu.VMEM((rows, d), BF16),
                        pltpu.VMEM((pages * page, d), BF16),
                        pltpu.VMEM((pages * page, d), BF16),
                        pltpu.VMEM((rows, LANE), F32),
                        pltpu.VMEM((rows, LANE), F32),
                        pltpu.VMEM((rows, d), F32)],
    )
    args = [page_table.reshape(-1), q] + [cache_k] * pages + [cache_v] * pages + [knew, vnew, lam_row, g_row]
    return pl.pallas_call(
        functools.partial(_paged_kernel, n_heads=n_heads, n_tok=n_tok, pages=pages, page=page,
                          lam_init=lam_init),
        out_shape=jax.ShapeDtypeStruct((bd, n_tok, d), F32),
        grid_spec=grid_spec,
        compiler_params=_cparams("parallel", "arbitrary"),
        name="paged_attention",
    )(*args)


def _mlstm_kernel(qk_ref, v_ref, om_ref, gcol_ref, grow_ref, cinit_ref, c0_ref, n0_ref, m0_ref,
                  cw_ref, cb_ref, gn_ref, y_ref, c_ref, n_ref, m_ref, xpad_ref, *, n_heads):
    ci = pl.program_id(1)
    t = qk_ref.shape[0]
    dm = v_ref.shape[1]
    dh = dm // n_heads
    hist = CONV_W - 1

    @pl.when(ci == 0)
    def _init():
        xpad_ref[0:8, :] = cinit_ref[...]
        c_ref[...] = c0_ref[...]
        n_ref[...] = n0_ref[...]
        m_ref[...] = m0_ref[...]

    x = qk_ref[...]
    xpad_ref[8:8 + t, :] = x
    y = cb_ref[...] + cw_ref[hist:hist + 1, :] * x
    for j in range(hist):
        y = y + cw_ref[j:j + 1, :] * xpad_ref[8 - hist + j:8 - hist + j + t, :]
    xpad_ref[0:8, :] = x[t - 8:t, :]
    y = y * _sigmoid(y)
    q_all = y[:, :dm].astype(BF16)
    k_all = (y[:, dm:] * (dh ** -0.5)).astype(BF16)

    gcol = gcol_ref[...]
    grow = grow_ref[...]
    r = lax.broadcasted_iota(jnp.int32, (t, t), 0)
    c = lax.broadcasted_iota(jnp.int32, (t, t), 1)
    causal = c <= r
    tril = jnp.where(causal, 1.0, 0.0).astype(F32)
    triu = jnp.where(r <= c, 1.0, 0.0).astype(F32)
    bc_all = jnp.dot(tril, gcol, preferred_element_type=F32, precision=lax.Precision.HIGHEST)
    br_all = jnp.dot(grow, triu, preferred_element_type=F32, precision=lax.Precision.HIGHEST)

    for h in range(n_heads):
        sl = slice(h * dh, (h + 1) * dh)
        qh, kh, vh = q_all[:, sl], k_all[:, sl], v_ref[:, sl]
        m_prev = m_ref[h][:, 0:1]
        b_c = bc_all[:, n_heads + h:n_heads + h + 1]
        li_c = gcol[:, h:h + 1]
        b_r = br_all[n_heads + h:n_heads + h + 1, :]
        li_r = grow[h:h + 1, :]
        log_d = jnp.where(causal, b_c - b_r + li_r, NEG)
        m_t = jnp.maximum(b_c + m_prev, jnp.max(log_d, axis=1, keepdims=True))
        s = _dot_nt(qh, kh) * jnp.exp(log_d - m_t)
        w_inter = jnp.exp(b_c + m_prev - m_t)
        ch = c_ref[h]
        num = _dot(s.astype(BF16), vh) + w_inter * _dot_nt(qh, ch.astype(BF16))
        nh = n_ref[h]
        den = jnp.sum(s, axis=1, keepdims=True) + w_inter * jnp.sum(qh.astype(F32) * nh, axis=1, keepdims=True)
        hout = num / jnp.maximum(jnp.abs(den), jnp.exp(-m_t))
        hn = hout * lax.rsqrt(jnp.mean(hout * hout, axis=1, keepdims=True) + EPS) * gn_ref[:, sl]
        y_ref[:, sl] = (_sigmoid(om_ref[:, sl].astype(F32)) * hn).astype(BF16)
        b_last = b_r[:, t - 1:t]
        log_w_r = b_last - b_r + li_r
        m_new = jnp.maximum(b_last + m_prev, jnp.max(log_w_r, axis=1, keepdims=True))
        w_c = jnp.exp(b_last - b_c + li_c - m_new)
        decay = jnp.exp(b_last + m_prev - m_new)
        wv = (w_c * vh.astype(F32)).astype(BF16)
        c_ref[h] = decay * ch + _dot_tn(wv, kh)
        n_ref[h] = decay * nh + jnp.sum(w_c * kh.astype(F32), axis=0, keepdims=True)
        m_ref[h] = jnp.broadcast_to(m_new, (1, LANE))


def _mlstm(qk, v, om, gcol, grow, conv_init, c0, n0, m0, conv_w, conv_b, g_norm, chunk, shared_init):
    bs, seq, dm2 = qk.shape
    dm = dm2 // 2
    n_heads, dh = c0.shape[1], c0.shape[2]
    ng = gcol.shape[2]
    nc = seq // chunk
    sidx = (lambda b: 0) if shared_init else (lambda b: b)
    row = lambda w: pl.BlockSpec((None, chunk, w), lambda b, i: (b, i, 0))
    return pl.pallas_call(
        functools.partial(_mlstm_kernel, n_heads=n_heads),
        out_shape=(jax.ShapeDtypeStruct((bs, seq, dm), BF16),
                   jax.ShapeDtypeStruct((bs, n_heads, dh, dh), F32),
                   jax.ShapeDtypeStruct((bs, n_heads, 1, dh), F32),
                   jax.ShapeDtypeStruct((bs, n_heads, 1, LANE), F32)),
        grid=(bs, nc),
        in_specs=[row(dm2), row(dm), row(dm), row(ng),
                  pl.BlockSpec((None, ng, chunk), lambda b, i: (b, 0, i)),
                  pl.BlockSpec((None, 8, dm2), lambda b, i: (sidx(b), 0, 0)),
                  pl.BlockSpec((None, n_heads, dh, dh), lambda b, i: (sidx(b), 0, 0, 0)),
                  pl.BlockSpec((None, n_heads, 1, dh), lambda b, i: (sidx(b), 0, 0, 0)),
                  pl.BlockSpec((None, n_heads, 1, LANE), lambda b, i: (sidx(b), 0, 0, 0)),
                  pl.BlockSpec((CONV_W, dm2), lambda b, i: (0, 0)),
                  pl.BlockSpec((1, dm2), lambda b, i: (0, 0)),
                  pl.BlockSpec((1, dm), lambda b, i: (0, 0))],
        out_specs=(row(dm),
                   pl.BlockSpec((None, n_heads, dh, dh), lambda b, i: (b, 0, 0, 0)),
                   pl.BlockSpec((None, n_heads, 1, dh), lambda b, i: (b, 0, 0, 0)),
                   pl.BlockSpec((None, n_heads, 1, LANE), lambda b, i: (b, 0, 0, 0))),
        scratch_shapes=[pltpu.VMEM((chunk + 8, dm2), F32)],
        compiler_params=_cparams("parallel", "arbitrary"),
        name="mlstm",
    )(qk, v, om, gcol, grow, conv_init, c0, n0, m0, conv_w, conv_b, g_norm)


def _layer_norm(z, g, b):
    mu = jnp.mean(z, axis=1, keepdims=True)
    zc = z - mu
    var = jnp.mean(zc * zc, axis=1, keepdims=True)
    return zc * lax.rsqrt(var + EPS) * g + b


def _merge_kernel(x_ref, ya_ref, ym_ref, ga_ref, gm_ref, wba_ref, wbm_ref, wout_ref, g_ref, b_ref,
                  wrh_ref, wrl_ref, h_ref, hrow_ref, s_ref, *, head_major):
    tm, d = x_ref.shape
    if head_major:
        ya = jnp.concatenate([ya_ref[h] for h in range(ya_ref.shape[0])], axis=1)
    else:
        ya = ya_ref[...]
    a = _dot(ya, wba_ref[...])
    m = _dot(ym_ref[...], wbm_ref[...])
    mix = ga_ref[...].astype(F32) * a + gm_ref[...].astype(F32) * m
    z = ALPHA * x_ref[...] + _dot(mix.astype(BF16), wout_ref[...])
    h = _layer_norm(z, g_ref[...], b_ref[...])
    h_ref[...] = h
    n_chunk = d // LANE
    for c in range(n_chunk):
        hrow_ref[pl.ds(c, tm, stride=n_chunk), :] = h[:, c * LANE:(c + 1) * LANE]
    hh = h.astype(BF16)
    hl = (h - hh.astype(F32)).astype(BF16)
    wrh = wrh_ref[...]
    s_ref[...] = _sigmoid(_dot(hh, wrh) + _dot(hl, wrh) + _dot(hh, wrl_ref[...]))


def _merge(x, ya, ym, ga, gm, wba, wbm, wout, g, b, wrh, wrl, tm, head_major):
    m, d = x.shape
    ne = wrh.shape[1]
    row = pl.BlockSpec((tm, d), lambda i: (i, 0))
    full = lambda a: pl.BlockSpec(a.shape, lambda i: (0,) * a.ndim)
    ya_spec = pl.BlockSpec((ya.shape[0], tm, LANE), lambda i: (0, i, 0)) if head_major else row
    return pl.pallas_call(
        functools.partial(_merge_kernel, head_major=head_major),
        out_shape=(jax.ShapeDtypeStruct((m, d), F32), jax.ShapeDtypeStruct((m * (d // LANE), LANE), F32),
                   jax.ShapeDtypeStruct((m, ne), F32)),
        grid=(m // tm,),
        in_specs=[row, ya_spec, row, row, row, full(wba), full(wbm), full(wout), full(g), full(b),
                  full(wrh), full(wrl)],
        out_specs=(row, pl.BlockSpec((tm * (d // LANE), LANE), lambda i: (i, 0)),
                   pl.BlockSpec((tm, ne), lambda i: (i, 0))),
        compiler_params=_cparams("parallel"),
        name="merge_ln_router",
    )(x, ya, ym, ga, gm, wba, wbm, wout, g, b, wrh, wrl)


def _route_kernel(s_ref, b_ref, idx_ref, rank_ref, gate_ref, cnt_ref, carry_ref, *, n_exp):
    i = pl.program_id(0)

    @pl.when(i == 0)
    def _init():
        carry_ref[...] = jnp.zeros(carry_ref.shape, F32)

    s = s_ref[...]
    tm = s.shape[0]
    ssel = s + b_ref[...]
    gsz = n_exp // N_GROUPS
    ninf = -jnp.inf
    lane_i = lax.broadcasted_iota(jnp.int32, (tm, n_exp), 1)
    lane = lane_i.astype(F32)
    lgrp = lane_i // gsz
    slot = lax.broadcasted_iota(jnp.int32, (tm, LANE), 1)

    grp = []
    for g in range(N_GROUPS):
        mg = jnp.where(lgrp == g, ssel, ninf)
        m1 = jnp.max(mg, axis=1, keepdims=True)
        i1 = jnp.min(jnp.where(mg == m1, lane, float(n_exp)), axis=1, keepdims=True)
        m2 = jnp.max(jnp.where(lane == i1, ninf, mg), axis=1, keepdims=True)
        grp.append(m1 + m2)
    enabled = jnp.zeros((tm, n_exp), F32)
    for g in range(N_GROUPS):
        ahead = jnp.zeros((tm, 1), F32)
        for j in range(N_GROUPS):
            if j != g:
                beats = (grp[j] > grp[g]) if j > g else (grp[j] >= grp[g])
                ahead = ahead + jnp.where(beats, 1.0, 0.0)
        keep = jnp.where(ahead < TOPK_GROUPS, 1.0, 0.0)
        enabled = jnp.where(lgrp == g, keep, enabled)
    cur = jnp.where(enabled > 0.5, ssel, ninf)

    sel = jnp.zeros((tm, n_exp), F32)
    idx_out = jnp.zeros((tm, LANE), F32)
    gate_out = jnp.zeros((tm, LANE), F32)
    gsum = jnp.zeros((tm, 1), F32)
    picks = []
    for k in range(TOP_K):
        m = jnp.max(cur, axis=1, keepdims=True)
        ik = jnp.min(jnp.where(cur == m, lane, float(n_exp)), axis=1, keepdims=True)
        pick = lane == ik
        gk = jnp.sum(jnp.where(pick, s, 0.0), axis=1, keepdims=True)
        cur = jnp.where(pick, ninf, cur)
        sel = jnp.where(pick, 1.0, sel)
        idx_out = jnp.where(slot == k, ik, idx_out)
        gate_out = jnp.where(slot == k, gk, gate_out)
        gsum = gsum + gk
        picks.append(ik)
    idx_ref[...] = idx_out
    gate_ref[...] = gate_out / gsum * ROUTE_SCALE

    r = lax.broadcasted_iota(jnp.int32, (tm, tm), 0)
    c = lax.broadcasted_iota(jnp.int32, (tm, tm), 1)
    ltri = jnp.where(c < r, 1.0, 0.0).astype(BF16)
    rank_dense = carry_ref[...] + _dot(ltri, sel.astype(BF16))
    rank_out = jnp.zeros((tm, LANE), F32)
    for k in range(TOP_K):
        rk = jnp.sum(jnp.where(lane == picks[k], rank_dense, 0.0), axis=1, keepdims=True)
        rank_out = jnp.where(slot == k, rk, rank_out)
    rank_ref[...] = rank_out
    total = carry_ref[...] + jnp.sum(sel, axis=0, keepdims=True)
    carry_ref[...] = total
    cnt_ref[...] = jnp.broadcast_to(total, cnt_ref.shape)


def _route(s, b_row, tm):
    t, n_exp = s.shape
    tok = jax.ShapeDtypeStruct((t, LANE), F32)
    tspec = pl.BlockSpec((tm, LANE), lambda i: (i, 0))
    return pl.pallas_call(
        functools.partial(_route_kernel, n_exp=n_exp),
        out_shape=(tok, tok, tok, jax.ShapeDtypeStruct((8, n_exp), F32)),
        grid=(t // tm,),
        in_specs=[pl.BlockSpec((tm, n_exp), lambda i: (i, 0)), pl.BlockSpec((1, n_exp), lambda i: (0, 0))],
        out_specs=(tspec, tspec, tspec, pl.BlockSpec((8, n_exp), lambda i: (0, 0))),
        scratch_shapes=[pltpu.VMEM((1, n_exp), F32)],
        compiler_params=_cparams("arbitrary"),
        name="route",
    )(s, b_row)


def _row_copy(src_ref, s0, dst_ref, d0, rows, sem):
    return pltpu.make_async_copy(src_ref.at[pl.ds(pl.multiple_of(s0, 8), rows)],
                                 dst_ref.at[pl.ds(pl.multiple_of(d0, 8), rows)], sem)


RANK_BITS = 20


def _sorted_row(starts_ref, code, rows):
    return (starts_ref[code >> RANK_BITS] + (code & ((1 << RANK_BITS) - 1))) * rows


def _scatter_kernel(starts_ref, code_ref, src_ref, dst_ref, sem, *, n_tok, fan, rows):
    def issue(t, carry):
        for u in range(fan):
            d0 = _sorted_row(starts_ref, code_ref[0, t * fan + u], rows)
            _row_copy(src_ref, t * rows, dst_ref, d0, rows, sem).start()
        return carry
    lax.fori_loop(0, n_tok, issue, 0)

    def drain(t, carry):
        for u in range(fan):
            _row_copy(src_ref, 0, dst_ref, 0, rows, sem).wait()
        return carry
    lax.fori_loop(0, n_tok, drain, 0)


def _scatter_rows(src, starts, code, n_dst_rows, n_tok, fan, rows):
    steps = src.shape[0] // (n_tok * rows)
    grid_spec = pltpu.PrefetchScalarGridSpec(
        num_scalar_prefetch=1,
        grid=(steps,),
        in_specs=[pl.BlockSpec((None, 1, n_tok * fan), lambda i, st: (i, 0, 0), memory_space=pltpu.SMEM),
                  pl.BlockSpec((n_tok * rows, src.shape[1]), lambda i, st: (i, 0))],
        out_specs=pl.BlockSpec(memory_space=pl.ANY),
        scratch_shapes=[pltpu.SemaphoreType.DMA(())],
    )
    return pl.pallas_call(
        functools.partial(_scatter_kernel, n_tok=n_tok, fan=fan, rows=rows),
        out_shape=jax.ShapeDtypeStruct((n_dst_rows, src.shape[1]), src.dtype),
        grid_spec=grid_spec,
        compiler_params=_cparams("arbitrary"),
        name="scatter_rows",
    )(starts, code.reshape(steps, 1, n_tok * fan), src)


def _expert_kernel(win_ref, exp_ref, lo_ref, hi_ref, x_ref, w1_ref, w3_ref, w2_ref, y_ref,
                   w1b_ref, w3b_ref, w2b_ref, xb_ref, *, blk, n_chunk):
    s = pl.program_id(0)
    sp = jnp.maximum(s - 1, 0)
    new_win = jnp.logical_or(s == 0, win_ref[s] != win_ref[sp])
    new_exp = jnp.logical_or(s == 0, exp_ref[s] != exp_ref[sp])
    lo = lo_ref[s]
    hi = hi_ref[s]

    @pl.when(new_win)
    def _load_rows():
        for c in range(n_chunk):
            xb_ref[:, c * LANE:(c + 1) * LANE] = x_ref[pl.ds(c, blk, stride=n_chunk), :].astype(BF16)

    @pl.when(new_exp)
    def _cast():
        w1b_ref[...] = w1_ref[...].astype(BF16)
        w3b_ref[...] = w3_ref[...].astype(BF16)
        w2b_ref[...] = w2_ref[...].astype(BF16)

    def run(first_visit):
        x = xb_ref[...]
        a = _dot(x, w1b_ref[...])
        g = (a * _sigmoid(a)) * _dot(x, w3b_ref[...])
        y = _dot(g.astype(BF16), w2b_ref[...])
        row = lax.broadcasted_iota(jnp.int32, (blk, LANE), 0)
        mine = jnp.logical_and(row >= lo, row < hi)
        for c in range(n_chunk):
            sl = pl.ds(c, blk, stride=n_chunk)
            other = 0.0 if first_visit else y_ref[sl, :]
            y_ref[sl, :] = jnp.where(mine, y[:, c * LANE:(c + 1) * LANE], other)

    @pl.when(jnp.logical_and(hi > lo, new_win))
    def _run_first():
        run(True)

    @pl.when(jnp.logical_and(hi > lo, jnp.logical_not(new_win)))
    def _run_again():
        run(False)


def _experts(xrow, win, exp, lo, hi, w1, w3, w2, blk):
    d, de = w1.shape[1], w1.shape[2]
    n_chunk = d // LANE
    n_steps = win.shape[0]
    xspec = pl.BlockSpec((blk * n_chunk, LANE), lambda s, wn, ex, l, h: (wn[s], 0))
    grid_spec = pltpu.PrefetchScalarGridSpec(
        num_scalar_prefetch=4,
        grid=(n_steps,),
        in_specs=[xspec,
                  pl.BlockSpec((None, d, de), lambda s, wn, ex, l, h: (ex[s], 0, 0)),
                  pl.BlockSpec((None, d, de), lambda s, wn, ex, l, h: (ex[s], 0, 0)),
                  pl.BlockSpec((None, de, d), lambda s, wn, ex, l, h: (ex[s], 0, 0))],
        out_specs=xspec,
        scratch_shapes=[pltpu.VMEM((d, de), BF16), pltpu.VMEM((d, de), BF16), pltpu.VMEM((de, d), BF16),
                        pltpu.VMEM((blk, d), BF16)],
    )
    return pl.pallas_call(
        functools.partial(_expert_kernel, blk=blk, n_chunk=n_chunk),
        out_shape=jax.ShapeDtypeStruct(xrow.shape, F32),
        grid_spec=grid_spec,
        compiler_params=_cparams("arbitrary"),
        name="experts",
    )(win, exp, lo, hi, xrow, w1, w3, w2)


def _final_kernel(starts_ref, code_ref, code_next_ref, h_ref, gate_ref, yrow_ref, ws1_ref, ws3_ref, ws2_ref,
                  g_ref, b_ref, o_ref, ybuf_ref, sem, *, n_chunk):
    i = pl.program_id(0)
    slot = i % 2
    h = h_ref[...]
    tm = h.shape[0]

    def gather(cref, s):
        def issue(t, carry):
            for k in range(TOP_K):
                s0 = _sorted_row(starts_ref, cref[0, t * TOP_K + k], n_chunk)
                _row_copy(yrow_ref, s0, ybuf_ref.at[s], (k * tm + t) * n_chunk, n_chunk, sem.at[s]).start()
            return carry
        lax.fori_loop(0, tm, issue, 0)

    @pl.when(i == 0)
    def _prime():
        gather(code_ref, 0)

    @pl.when(i + 1 < pl.num_programs(0))
    def _prefetch():
        gather(code_next_ref, 1 - slot)

    yg_ref = ybuf_ref.at[slot]

    def drain(t, carry):
        for k in range(TOP_K):
            _row_copy(yrow_ref, 0, yg_ref, 0, n_chunk, sem.at[slot]).wait()
        return carry
    lax.fori_loop(0, tm, drain, 0)

    x = h.astype(BF16)
    a = _dot(x, ws1_ref[...])
    gg = (a * _sigmoid(a)) * _dot(x, ws3_ref[...])
    shared = _dot(gg.astype(BF16), ws2_ref[...])
    gate = gate_ref[...]
    gk = [jnp.broadcast_to(gate[:, k:k + 1], (tm, LANE)) for k in range(TOP_K)]
    chunks = []
    for c in range(n_chunk):
        acc = gk[0] * yg_ref[pl.ds(c, tm, stride=n_chunk), :]
        for k in range(1, TOP_K):
            acc = acc + gk[k] * yg_ref[pl.ds(k * tm * n_chunk + c, tm, stride=n_chunk), :]
        chunks.append(acc)
    f = jnp.concatenate(chunks, axis=1) + shared
    o_ref[...] = _layer_norm(ALPHA * h + f, g_ref[...], b_ref[...])


def _final(h, yrow, starts, code, gate, ws1, ws3, ws2, g, b, tm, blk_off):
    m, d = h.shape
    n_chunk = d // LANE
    n_blk = m // tm
    row = pl.BlockSpec((tm, d), lambda i, st: (i, 0))
    full = lambda a: pl.BlockSpec(a.shape, lambda i, st: (0,) * a.ndim)
    cspec = lambda nxt: pl.BlockSpec((None, 1, tm * TOP_K),
                                     lambda i, st: (jnp.minimum(i + nxt, n_blk - 1) + blk_off, 0, 0),
                                     memory_space=pltpu.SMEM)
    grid_spec = pltpu.PrefetchScalarGridSpec(
        num_scalar_prefetch=1,
        grid=(n_blk,),
        in_specs=[cspec(0), cspec(1), row, pl.BlockSpec((tm, LANE), lambda i, st: (i + blk_off, 0)),
                  pl.BlockSpec(memory_space=pl.ANY),
                  full(ws1), full(ws3), full(ws2), full(g), full(b)],
        out_specs=row,
        scratch_shapes=[pltpu.VMEM((2, tm * TOP_K * n_chunk, LANE), F32), pltpu.SemaphoreType.DMA((2,))],
    )
    code3 = code.reshape(-1, 1, tm * TOP_K)
    return pl.pallas_call(
        functools.partial(_final_kernel, n_chunk=n_chunk),
        out_shape=jax.ShapeDtypeStruct((m, d), F32),
        grid_spec=grid_spec,
        compiler_params=_cparams("arbitrary"),
        name="shared_ln",
    )(starts, code3, code3, h, gate, yrow, ws1, ws3, ws2, g, b)


def _rope_tables(pos):
    half = ROPE_DIM // 2
    inv = ROPE_THETA ** (-jnp.arange(half, dtype=F32) * 2.0 / ROPE_DIM)
    ang = pos.astype(F32)[:, None] * inv
    cos8, sin8 = jnp.cos(ang), jnp.sin(ang)
    n = pos.shape[0]
    cos = jnp.concatenate([cos8, cos8, jnp.ones((n, D_QK - ROPE_DIM), F32)], axis=1)
    sin = jnp.concatenate([-sin8, sin8, jnp.zeros((n, D_QK - ROPE_DIM), F32)], axis=1)
    return jnp.tile(cos, (1, LANE // D_QK)), jnp.tile(sin, (1, LANE // D_QK))


def _expert_steps(starts, ends, n_win, blk):
    n_exp = starts.shape[0]
    first_w = starts // blk
    n_w = jnp.where(ends > starts, (ends + blk - 1) // blk - first_w, 0)
    step_end = jnp.cumsum(n_w)
    step_start = step_end - n_w
    total = step_end[-1]
    s = jnp.arange(n_win + n_exp, dtype=jnp.int32)
    e = jnp.minimum(jnp.searchsorted(step_end, s, side='right'), n_exp - 1).astype(jnp.int32)
    w = first_w[e] + (s - step_start[e])
    lo = jnp.maximum(starts[e], w * blk) - w * blk
    hi = jnp.minimum(ends[e], (w + 1) * blk) - w * blk
    valid = s < total
    last = jnp.maximum(total - 1, 0)
    e = jnp.where(valid, e, e[last])
    w = jnp.where(valid, w, w[last])
    lo = jnp.where(valid, lo, 0)
    hi = jnp.where(valid, hi, 0)
    return w.astype(jnp.int32), e, lo.astype(jnp.int32), hi.astype(jnp.int32)


def _row_tile(m, pref):
    t = pref
    while m % t:
        t //= 2
    return t


def kernel(x_prompt, x_sample, cache_k, cache_v, page_table, state_C, state_n, state_m, state_conv, meta_tokens, w_in, b_in, lam_q1, lam_k1, lam_q2, lam_k2, g_subln, conv_w, conv_b, g_mnorm, w_ba, w_bm, w_out, ln1_g, ln1_b, w_router, b_router, w1, w3, w2, ws1, ws3, ws2, ln2_g, ln2_b):
    B, SEQ, D = x_prompt.shape
    BD, S, _ = x_sample.shape
    NHA, DVA = cache_k.shape[3], cache_v.shape[4]
    NHM, DHM = state_C.shape[2], state_C.shape[3]
    n_exp = w_router.shape[2]
    n_pages = page_table.shape[1]
    page = cache_k.shape[2]
    l = 0
    lam_init = 0.8 - 0.6 * math.exp(-0.3 * l)
    lam = (jnp.exp(jnp.sum(lam_q1[l] * lam_k1[l])) - jnp.exp(jnp.sum(lam_q2[l] * lam_k2[l])) + lam_init)
    lam_row = jnp.broadcast_to(lam.astype(F32), (1, LANE))
    g_row = g_subln[l].astype(F32).reshape(1, LANE)

    w = w_in[l]
    bias = b_in[l].astype(F32)
    o_mix = 3 * D
    o_if = o_mix + 4 * D
    o_g = o_if + 2 * NHM
    w_qkv = w[:, :o_mix].astype(BF16)
    w_mix = w[:, o_mix:o_if].astype(BF16)
    w_g = w[:, o_g:].astype(BF16)
    w_if = w[:, o_if:o_g]
    b_qkv = bias[None, :o_mix]
    b_mix = bias[None, o_mix:o_if]
    b_g = bias[None, o_g:]
    b_if = bias[o_if:o_g]

    MP = B * SEQ
    xp = x_prompt.reshape(MP, D)
    xs = x_sample.reshape(BD * S, D)
    xm = jnp.zeros((PAD_CHUNK, D), F32).at[:N_META].set(meta_tokens.astype(F32))
    tmp = _row_tile(SEQ, 512)
    tms = _row_tile(BD * S, 256)

    cos_p, sin_p = _rope_tables(N_META + jnp.arange(SEQ))
    cos_s, sin_s = _rope_tables(jnp.tile(n_pages * page + jnp.arange(S), BD))
    cos_m, sin_m = _rope_tables(jnp.arange(PAD_CHUNK))

    xp_bf, xs_bf, xm_bf = xp.astype(BF16), xs.astype(BF16), xm.astype(BF16)
    qlo_p, qhi_p, kf_p, kb_p, vf_p, vb_p = _proj_qkv(xp_bf, w_qkv, b_qkv, cos_p, sin_p, tmp, NHA)
    qlo_s, qhi_s, kf_s, _, vf_s, _ = _proj_qkv(xs_bf, w_qkv, b_qkv, cos_s, sin_s, tms, NHA)
    _, _, kf_m, kb_m, vf_m, vb_m = _proj_qkv(xm_bf, w_qkv, b_qkv, cos_m, sin_m, PAD_CHUNK, NHA)

    qk_p, vm_p, om_p = _proj_mix(xp_bf, w_mix, b_mix, tmp)
    qk_s, vm_s, om_s = _proj_mix(xs_bf, w_mix, b_mix, tms)
    qk_m, vm_m, om_m = _proj_mix(xm_bf, w_mix, b_mix, PAD_CHUNK)

    ga_p, gm_p = _proj_gates(xp_bf, w_g, b_g, tmp)
    ga_s, gm_s = _proj_gates(xs_bf, w_g, b_g, tms)

    if_p = _proj_if(xp, w_if, b_if, tmp, NHM)[:, :2 * NHM]
    if_s = _proj_if(xs, w_if, b_if, tms, NHM)[:, :2 * NHM]
    if_m = _proj_if(xm, w_if, b_if, PAD_CHUNK, NHM)[:, :2 * NHM]

    ya_p = _prompt_attention(qlo_p, qhi_p, kb_p, vb_p, kb_m, vb_m, lam_row, g_row, B, lam_init,
                             _row_tile(SEQ, ATTN_TILE))
    q_s = jnp.transpose(qlo_s + qhi_s, (1, 0, 2)).reshape(BD, S, D).astype(F32)
    kn = jnp.zeros((BD, LANE, D), BF16).at[:, :S].set(kf_s.reshape(BD, S, D).astype(BF16))
    vn = jnp.zeros((BD, LANE, D), BF16).at[:, :S].set(vf_s.reshape(BD, S, D).astype(BF16))
    n_pool = cache_k.shape[1]
    ya_s = _paged_attention(q_s, kn, vn, cache_k.reshape(-1, page * NHA, DVA), cache_v.reshape(-1, page * NHA, DVA),
                            page_table + l * n_pool, lam_row, g_row, NHA, lam_init,
                            math.gcd(PAGES_PER_STEP, n_pages))

    cw = conv_w[l].astype(F32)
    cb = conv_b[l].astype(F32)[None]
    gn = g_mnorm[l].astype(F32)[None]
    pad_gate = jnp.concatenate([jnp.full((NHM,), NEG, F32), jnp.zeros((NHM,), F32)])

    def gate_layouts(g, bs, real, total):
        g = g.reshape(bs, -1, 2 * NHM)[:, :real]
        if total > real:
            g = jnp.concatenate([g, jnp.broadcast_to(pad_gate, (bs, total - real, 2 * NHM))], axis=1)
        return g, jnp.transpose(g, (0, 2, 1))

    def pad_rows(a, bs, real, total):
        a = a.reshape(bs, -1, a.shape[-1])[:, :real]
        return jnp.pad(a, ((0, 0), (0, total - real), (0, 0)))

    gcol_m, grow_m = gate_layouts(if_m, 1, N_META, PAD_CHUNK)
    zC = jnp.zeros((1, NHM, DHM, DHM), F32)
    zn = jnp.zeros((1, NHM, 1, DHM), F32)
    zm = jnp.zeros((1, NHM, 1, LANE), F32)
    _, c_m, n_m, m_m = _mlstm(pad_rows(qk_m, 1, N_META, PAD_CHUNK), pad_rows(vm_m, 1, N_META, PAD_CHUNK),
                              pad_rows(om_m, 1, N_META, PAD_CHUNK), gcol_m, grow_m,
                              jnp.zeros((1, 8, 2 * D), F32), zC, zn, zm, cw, cb, gn, PAD_CHUNK, True)
    hist_m = jnp.zeros((1, 8, 2 * D), F32).at[0, 8 - (CONV_W - 1):].set(qk_m[N_META - (CONV_W - 1):N_META])
    gcol_p, grow_p = gate_layouts(if_p, B, SEQ, SEQ)
    ym_p, c_p, n_p, m_p = _mlstm(qk_p.reshape(B, SEQ, 2 * D), vm_p.reshape(B, SEQ, D), om_p.reshape(B, SEQ, D),
                                 gcol_p, grow_p, hist_m, c_m, n_m, m_m, cw, cb, gn,
                                 _row_tile(SEQ, MLSTM_CHUNK), True)
    gcol_s, grow_s = gate_layouts(if_s, BD, S, PAD_CHUNK)
    hist_s = jnp.zeros((BD, 8, 2 * D), F32).at[:, 8 - (CONV_W - 1):].set(state_conv[l].astype(F32))
    ym_s, c_s, n_s, m_s = _mlstm(pad_rows(qk_s, BD, S, PAD_CHUNK), pad_rows(vm_s, BD, S, PAD_CHUNK),
                                 pad_rows(om_s, BD, S, PAD_CHUNK), gcol_s, grow_s, hist_s,
                                 state_C[l].astype(F32), state_n[l].astype(F32)[:, :, None, :],
                                 jnp.broadcast_to(state_m[l].astype(F32)[:, :, None, None], (BD, NHM, 1, LANE)),
                                 cw, cb, gn, PAD_CHUNK, False)
    ym_s = ym_s[:, :S].reshape(BD * S, D)

    wba, wbm, wout = w_ba[l].astype(BF16), w_bm[l].astype(BF16), w_out[l].astype(BF16)
    g1, b1 = ln1_g[l].astype(F32)[None], ln1_b[l].astype(F32)[None]
    wr = w_router[l].astype(F32)
    wrh = wr.astype(BF16)
    wrl = (wr - wrh.astype(F32)).astype(BF16)
    h_p, hrow_p, s_p = _merge(xp, ya_p, ym_p.reshape(MP, D), ga_p, gm_p, wba, wbm, wout, g1, b1, wrh, wrl,
                              _row_tile(MP, 256), True)
    h_s, hrow_s, s_s = _merge(xs, ya_s.reshape(BD * S, D).astype(BF16), ym_s, ga_s, gm_s, wba, wbm, wout,
                              g1, b1, wrh, wrl, tms, False)

    MT = MP + BD * S
    n_chunk = D // LANE
    tk = MT * TOP_K
    tt = _row_tile(math.gcd(MP, BD * S), 256)
    hrow = jnp.concatenate([hrow_p, hrow_s], axis=0)
    s_all = jnp.concatenate([s_p, s_s], axis=0)
    idx_f, rank_f, gate, cnt = _route(s_all, b_router[l].astype(F32)[None], tt)
    idx = idx_f[:, :TOP_K].astype(jnp.int32)
    counts = cnt[0].astype(jnp.int32)
    ends = jnp.cumsum(counts)
    starts = ends - counts
    assert MT < (1 << RANK_BITS)
    code = ((idx << RANK_BITS) | rank_f[:, :TOP_K].astype(jnp.int32)).reshape(tk)
    n_win = -(-tk // MOE_BLOCK)
    xrow = _scatter_rows(hrow, starts, code, n_win * MOE_BLOCK * n_chunk, tt, TOP_K, n_chunk)
    win, exp, lo, hi = _expert_steps(starts, ends, n_win, MOE_BLOCK)
    yrow = _experts(xrow, win, exp, lo, hi, w1[l], w3[l], w2[l], MOE_BLOCK)

    ws1b, ws3b, ws2b = ws1[l].astype(BF16), ws3[l].astype(BF16), ws2[l].astype(BF16)
    g2, b2 = ln2_g[l].astype(F32)[None], ln2_b[l].astype(F32)[None]
    y_p = _final(h_p, yrow, starts, code, gate, ws1b, ws3b, ws2b, g2, b2, tt, 0)
    y_s = _final(h_s, yrow, starts, code, gate, ws1b, ws3b, ws2b, g2, b2, tt, MP // tt)

    def with_meta(meta_rows, real):
        meta = jnp.broadcast_to(meta_rows[None, :N_META], (B, N_META, D))
        return jnp.concatenate([meta, real.reshape(B, SEQ, D)], axis=1).reshape(1, B, N_META + SEQ, NHA, DVA)

    k_prompt = with_meta(kf_m, kf_p)
    v_prompt = with_meta(vf_m, vf_p)
    conv_prompt = qk_p.reshape(B, SEQ, 2 * D)[:, SEQ - (CONV_W - 1):][None]
    conv_sample = qk_s.reshape(BD, S, 2 * D)[:, S - (CONV_W - 1):][None]
    return (y_p.reshape(B, SEQ, D), y_s.reshape(BD, S, D),
            k_prompt, v_prompt,
            c_p[None], n_p[:, :, 0][None], m_p[:, :, 0, 0][None], conv_prompt,
            kf_s.reshape(1, BD, S, NHA, DVA), vf_s.reshape(1, BD, S, NHA, DVA),
            c_s[None], n_s[:, :, 0][None], m_s[:, :, 0, 0][None], conv_sample)
```

```python
import functools
import math

import jax
import jax.numpy as jnp
from jax import lax
from jax.experimental import pallas as pl
from jax.experimental.pallas import tpu as pltpu

F32 = jnp.float32
BF16 = jnp.bfloat16

DEPTH = 1
N_META = 16
D_QK = 64
ROPE_DIM = D_QK // 4
ROPE_THETA = 500000.0
CONV_W = 4
TOP_K = 8
N_GROUPS = 8
TOPK_GROUPS = 4
ROUTE_SCALE = 2.5
ALPHA = (2 * DEPTH) ** 0.25
EPS = 1e-5
NEG = -1e30

LANE = 128
VMEM_LIMIT = 52 * 1024 * 1024

ATTN_TILE = 512
MLSTM_CHUNK = 256
PAD_CHUNK = 128
PAGES_PER_STEP = 16
MOE_BLOCK = 256


def _cparams(*sem):
    return pltpu.CompilerParams(dimension_semantics=sem, vmem_limit_bytes=VMEM_LIMIT)


def _dot(a, b):
    return jnp.dot(a, b, preferred_element_type=F32)


def _dot_nt(a, b):
    return lax.dot_general(a, b, (((1,), (1,)), ((), ())), preferred_element_type=F32)


def _dot_tn(a, b):
    return lax.dot_general(a, b, (((0,), (0,)), ((), ())), preferred_element_type=F32)


def _split3(a):
    a1 = a.astype(BF16)
    r1 = a - a1.astype(F32)
    a2 = r1.astype(BF16)
    a3 = (r1 - a2.astype(F32)).astype(BF16)
    return a1, a2, a3


def _sigmoid(x):
    return 1.0 / (1.0 + jnp.exp(-x))


def _qkv_kernel(x_ref, w_ref, b_ref, cos_ref, sin_ref,
                qlo_ref, qhi_ref, kf_ref, kb_ref, vf_ref, vb_ref, *, n_heads, d_model):
    x = x_ref[...]
    cos = cos_ref[...]
    sin = sin_ref[...]
    lane = lax.broadcasted_iota(jnp.int32, cos.shape, 1)
    first = (lane % D_QK) < (ROPE_DIM // 2)
    lo_half = lane < D_QK
    half = ROPE_DIM // 2

    def rope(s):
        up = pltpu.roll(s, LANE - half, axis=1)
        dn = pltpu.roll(s, half, axis=1)
        return s * cos + jnp.where(first, up, dn) * sin

    zq = _dot(x, w_ref[:, 0:d_model]) + b_ref[:, 0:d_model]
    for h in range(n_heads):
        q = rope(zq[:, h * LANE:(h + 1) * LANE]) * (D_QK ** -0.5)
        qlo_ref[h] = jnp.where(lo_half, q, 0.0).astype(BF16)
        qhi_ref[h] = jnp.where(lo_half, 0.0, q).astype(BF16)
    zk = _dot(x, w_ref[:, d_model:2 * d_model]) + b_ref[:, d_model:2 * d_model]
    for h in range(n_heads):
        k = rope(zk[:, h * LANE:(h + 1) * LANE])
        kf_ref[:, h * LANE:(h + 1) * LANE] = k
        kb_ref[h] = k.astype(BF16)
    zv = _dot(x, w_ref[:, 2 * d_model:3 * d_model]) + b_ref[:, 2 * d_model:3 * d_model]
    vf_ref[...] = zv
    for h in range(n_heads):
        vb_ref[h] = zv[:, h * LANE:(h + 1) * LANE].astype(BF16)


def _proj_qkv(x_bf, w_bf, bias, cos, sin, tm, n_heads):
    m, d = x_bf.shape
    n_pos = cos.shape[0] // tm
    hm = jax.ShapeDtypeStruct((n_heads, m, LANE), BF16)
    fm = jax.ShapeDtypeStruct((m, d), F32)
    hspec = pl.BlockSpec((n_heads, tm, LANE), lambda i: (0, i, 0))
    fspec = pl.BlockSpec((tm, d), lambda i: (i, 0))
    pspec = pl.BlockSpec((tm, LANE), lambda i: (i % n_pos, 0))
    return pl.pallas_call(
        functools.partial(_qkv_kernel, n_heads=n_heads, d_model=d),
        out_shape=(hm, hm, fm, hm, fm, hm),
        grid=(m // tm,),
        in_specs=[fspec, pl.BlockSpec((d, 3 * d), lambda i: (0, 0)),
                  pl.BlockSpec((1, 3 * d), lambda i: (0, 0)), pspec, pspec],
        out_specs=(hspec, hspec, fspec, hspec, fspec, hspec),
        compiler_params=_cparams("parallel"),
        name="proj_qkv",
    )(x_bf, w_bf, bias, cos, sin)


def _mix_proj_kernel(x_ref, w_ref, b_ref, qk_ref, v_ref, om_ref, *, d_model):
    x = x_ref[...]
    d = d_model
    qk_ref[...] = _dot(x, w_ref[:, 0:2 * d]) + b_ref[:, 0:2 * d]
    v_ref[...] = (_dot(x, w_ref[:, 2 * d:3 * d]) + b_ref[:, 2 * d:3 * d]).astype(BF16)
    om_ref[...] = (_dot(x, w_ref[:, 3 * d:4 * d]) + b_ref[:, 3 * d:4 * d]).astype(BF16)


def _proj_mix(x_bf, w_bf, bias, tm):
    m, d = x_bf.shape
    return pl.pallas_call(
        functools.partial(_mix_proj_kernel, d_model=d),
        out_shape=(jax.ShapeDtypeStruct((m, 2 * d), F32), jax.ShapeDtypeStruct((m, d), BF16),
                   jax.ShapeDtypeStruct((m, d), BF16)),
        grid=(m // tm,),
        in_specs=[pl.BlockSpec((tm, d), lambda i: (i, 0)),
                  pl.BlockSpec((d, 4 * d), lambda i: (0, 0)),
                  pl.BlockSpec((1, 4 * d), lambda i: (0, 0))],
        out_specs=(pl.BlockSpec((tm, 2 * d), lambda i: (i, 0)), pl.BlockSpec((tm, d), lambda i: (i, 0)),
                   pl.BlockSpec((tm, d), lambda i: (i, 0))),
        compiler_params=_cparams("parallel"),
        name="proj_mix",
    )(x_bf, w_bf, bias)


def _gate_proj_kernel(x_ref, w_ref, b_ref, ga_ref, gm_ref, *, d_model):
    x = x_ref[...]
    d = d_model
    ga_ref[...] = _sigmoid(_dot(x, w_ref[:, 0:d]) + b_ref[:, 0:d]).astype(BF16)
    gm_ref[...] = _sigmoid(_dot(x, w_ref[:, d:2 * d]) + b_ref[:, d:2 * d]).astype(BF16)


def _proj_gates(x_bf, w_bf, bias, tm):
    m, d = x_bf.shape
    return pl.pallas_call(
        functools.partial(_gate_proj_kernel, d_model=d),
        out_shape=(jax.ShapeDtypeStruct((m, d), BF16), jax.ShapeDtypeStruct((m, d), BF16)),
        grid=(m // tm,),
        in_specs=[pl.BlockSpec((tm, d), lambda i: (i, 0)),
                  pl.BlockSpec((d, 2 * d), lambda i: (0, 0)),
                  pl.BlockSpec((1, 2 * d), lambda i: (0, 0))],
        out_specs=(pl.BlockSpec((tm, d), lambda i: (i, 0)), pl.BlockSpec((tm, d), lambda i: (i, 0))),
        compiler_params=_cparams("parallel"),
        name="proj_gates",
    )(x_bf, w_bf, bias)


def _if_proj_kernel(x_ref, wh_ref, wl_ref, b_ref, o_ref, *, n_heads):
    x = x_ref[...]
    xh = x.astype(BF16)
    xl = (x - xh.astype(F32)).astype(BF16)
    wh = wh_ref[...]
    z = _dot(xh, wh) + _dot(xl, wh) + _dot(xh, wl_ref[...]) + b_ref[...]
    lane = lax.broadcasted_iota(jnp.int32, z.shape, 1)
    logsig = jnp.minimum(z, 0.0) - jnp.log(1.0 + jnp.exp(-jnp.abs(z)))
    o_ref[...] = jnp.where(lane < n_heads, z, logsig)


def _proj_if(x_f32, w_if, b_if, tm, n_heads):
    m, d = x_f32.shape
    wpad = jnp.zeros((d, LANE), F32).at[:, :2 * n_heads].set(w_if)
    bpad = jnp.zeros((1, LANE), F32).at[0, :2 * n_heads].set(b_if)
    wh = wpad.astype(BF16)
    wl = (wpad - wh.astype(F32)).astype(BF16)
    return pl.pallas_call(
        functools.partial(_if_proj_kernel, n_heads=n_heads),
        out_shape=jax.ShapeDtypeStruct((m, LANE), F32),
        grid=(m // tm,),
        in_specs=[pl.BlockSpec((tm, d), lambda i: (i, 0)),
                  pl.BlockSpec((d, LANE), lambda i: (0, 0)),
                  pl.BlockSpec((d, LANE), lambda i: (0, 0)),
                  pl.BlockSpec((1, LANE), lambda i: (0, 0))],
        out_specs=pl.BlockSpec((tm, LANE), lambda i: (i, 0)),
        compiler_params=_cparams("parallel"),
        name="proj_if",
    )(x_f32, wh, wl, bpad)


def _flash_update(idx, q, kvs, m_ref, l_ref, acc_ref, first):
    ss = []
    for k, _, mask in kvs:
        s = _dot_nt(q, k)
        ss.append(s if mask is None else jnp.where(mask, s, NEG))
    m_cur = functools.reduce(jnp.maximum, [jnp.max(s, axis=1, keepdims=True) for s in ss])
    if first:
        m_new = jnp.broadcast_to(m_cur, (q.shape[0], LANE))
    else:
        m_prev = m_ref[idx]
        m_new = jnp.maximum(m_prev, m_cur)
    lsum, pv = None, None
    for s, (_, v, _) in zip(ss, kvs):
        p = jnp.exp(s - jnp.tile(m_new, (1, s.shape[1] // LANE)))
        ls = jnp.sum(p, axis=1, keepdims=True)
        d = _dot(p.astype(BF16), v)
        lsum = ls if lsum is None else lsum + ls
        pv = d if pv is None else pv + d
    if first:
        l_ref[idx] = jnp.broadcast_to(lsum, (q.shape[0], LANE))
        acc_ref[idx] = pv
    else:
        alpha = jnp.exp(m_prev - m_new)
        l_ref[idx] = alpha * l_ref[idx] + lsum
        acc_ref[idx] = alpha * acc_ref[idx] + pv
    m_ref[idx] = m_new


def _attn_kernel(qi_tab, kj_tab, qlo_ref, qhi_ref, k_ref, v_ref, km_ref, vm_ref, lam_ref, g_ref,
                 o_ref, m_ref, l_ref, acc_ref, *, n_heads, lam_init):
    s_idx = pl.program_id(1)
    qi = qi_tab[s_idx]
    kj = kj_tab[s_idx]
    tq = qlo_ref.shape[1]
    tk = k_ref.shape[1]

    def head_loop(kvs_of, first, finish=None):
        def body(h, c):
            kvs = kvs_of(h)
            _flash_update(2 * h, qlo_ref[h], kvs, m_ref, l_ref, acc_ref, first)
            _flash_update(2 * h + 1, qhi_ref[h], kvs, m_ref, l_ref, acc_ref, first)
            if finish is not None:
                finish(h)
            return c
        lax.fori_loop(0, n_heads, body, 0, unroll=4)

    @pl.when(jnp.logical_and(kj == 0, qi > 0))
    def _first():
        head_loop(lambda h: [(k_ref[h], v_ref[h], None)], True)

    @pl.when(jnp.logical_and(kj > 0, kj < qi))
    def _full():
        head_loop(lambda h: [(k_ref[h], v_ref[h], None)], False)

    @pl.when(kj == qi)
    def _diag():
        @pl.when(qi == 0)
        def _init():
            m_ref[...] = jnp.full(m_ref.shape, NEG, F32)
            l_ref[...] = jnp.zeros(l_ref.shape, F32)
            acc_ref[...] = jnp.zeros(acc_ref.shape, F32)

        row = lax.broadcasted_iota(jnp.int32, (tq, tk), 0)
        col = lax.broadcasted_iota(jnp.int32, (tq, tk), 1)
        causal = col <= row
        is_meta = lax.broadcasted_iota(jnp.int32, (tq, km_ref.shape[1]), 1) < N_META
        lam = lam_ref[...]
        g = g_ref[...] * (1.0 - lam_init)

        def finish(h):
            o = acc_ref[2 * h] / l_ref[2 * h] - lam * (acc_ref[2 * h + 1] / l_ref[2 * h + 1])
            o = o * lax.rsqrt(jnp.mean(o * o, axis=1, keepdims=True) + EPS) * g
            o_ref[h] = o.astype(BF16)

        head_loop(lambda h: [(k_ref[h], v_ref[h], causal), (km_ref[h], vm_ref[h], is_meta)], False, finish)


def _prompt_attention(qlo, qhi, kb, vb, kmeta, vmeta, lam_row, g_row, batch, lam_init, tile):
    n_heads, m, _ = qlo.shape
    seq = m // batch
    nq = seq // tile
    qi_list, kj_list = [], []
    for qi in range(nq):
        for kj in range(qi + 1):
            qi_list.append(qi)
            kj_list.append(kj)
    qi_tab = jnp.asarray(qi_list, jnp.int32)
    kj_tab = jnp.asarray(kj_list, jnp.int32)
    n_steps = len(qi_list)
    tmeta = kmeta.shape[1]
    qspec = pl.BlockSpec((n_heads, tile, LANE), lambda b, s, qt, kt: (0, b * nq + qt[s], 0))
    kspec = pl.BlockSpec((n_heads, tile, LANE), lambda b, s, qt, kt: (0, b * nq + kt[s], 0))
    mspec = pl.BlockSpec((n_heads, tmeta, LANE), lambda b, s, qt, kt: (0, 0, 0))
    rspec = pl.BlockSpec((1, LANE), lambda b, s, qt, kt: (0, 0))
    grid_spec = pltpu.PrefetchScalarGridSpec(
        num_scalar_prefetch=2,
        grid=(batch, n_steps),
        in_specs=[qspec, qspec, kspec, kspec, mspec, mspec, rspec, rspec],
        out_specs=qspec,
        scratch_shapes=[pltpu.VMEM((2 * n_heads, tile, LANE), F32),
                        pltpu.VMEM((2 * n_heads, tile, LANE), F32),
                        pltpu.VMEM((2 * n_heads, tile, LANE), F32)],
    )
    return pl.pallas_call(
        functools.partial(_attn_kernel, n_heads=n_heads, lam_init=lam_init),
        out_shape=jax.ShapeDtypeStruct((n_heads, m, LANE), BF16),
        grid_spec=grid_spec,
        compiler_params=_cparams("parallel", "arbitrary"),
        name="prompt_attention",
    )(qi_tab, kj_tab, qlo, qhi, kb, vb, kmeta, vmeta, lam_row, g_row)


def _paged_kernel(pt_ref, q_ref, *refs, n_heads, n_tok, pages, page, lam_init):
    k_refs = refs[:pages]
    v_refs = refs[pages:2 * pages]
    kn_ref, vn_ref, lam_ref, g_ref, o_ref, qbd_ref, kb_ref, vb_ref, m_ref, l_ref, acc_ref = refs[2 * pages:]
    j = pl.program_id(1)
    rows = 2 * n_heads * n_tok
    d = q_ref.shape[1]

    @pl.when(j == 0)
    def _init():
        q = q_ref[...]
        qt = jnp.tile(q, (rows // n_tok, 1))
        r = lax.broadcasted_iota(jnp.int32, (rows, d), 0)
        c = lax.broadcasted_iota(jnp.int32, (rows, d), 1)
        qbd_ref[...] = jnp.where((c // D_QK) == (r // n_tok), qt, 0.0).astype(BF16)
        m_ref[...] = jnp.full(m_ref.shape, NEG, F32)
        l_ref[...] = jnp.zeros(l_ref.shape, F32)
        acc_ref[...] = jnp.zeros(acc_ref.shape, F32)

    def update(k, v, mask):
        s = _dot_nt(qbd_ref[...], k)
        if mask is not None:
            s = jnp.where(mask, s, NEG)
        m_prev = m_ref[...]
        m_new = jnp.maximum(m_prev, jnp.max(s, axis=1, keepdims=True))
        alpha = jnp.exp(m_prev - m_new)
        p = jnp.exp(s - jnp.tile(m_new, (1, s.shape[1] // LANE)))
        l_ref[...] = alpha * l_ref[...] + jnp.sum(p, axis=1, keepdims=True)
        acc_ref[...] = jnp.tile(alpha, (1, d // LANE)) * acc_ref[...] + _dot(p.astype(BF16), v)
        m_ref[...] = m_new

    for p in range(pages):
        for h in range(n_heads):
            kb_ref[p * page:(p + 1) * page, h * LANE:(h + 1) * LANE] = (
                k_refs[p][pl.ds(h, page, stride=n_heads), :].astype(BF16))
            vb_ref[p * page:(p + 1) * page, h * LANE:(h + 1) * LANE] = (
                v_refs[p][pl.ds(h, page, stride=n_heads), :].astype(BF16))
    update(kb_ref[...], vb_ref[...], None)

    @pl.when(j == pl.num_programs(1) - 1)
    def _final():
        r = lax.broadcasted_iota(jnp.int32, (rows, kn_ref.shape[0]), 0)
        c = lax.broadcasted_iota(jnp.int32, (rows, kn_ref.shape[0]), 1)
        update(kn_ref[...], vn_ref[...], c <= (r % n_tok))
        o = acc_ref[...] / jnp.tile(l_ref[...], (1, d // LANE))
        lam = lam_ref[...]
        g = g_ref[...] * (1.0 - lam_init)
        for h in range(n_heads):
            r0 = 2 * h * n_tok
            o1 = o[r0:r0 + n_tok, h * LANE:(h + 1) * LANE]
            o2 = o[r0 + n_tok:r0 + 2 * n_tok, h * LANE:(h + 1) * LANE]
            oh = o1 - lam * o2
            oh = oh * lax.rsqrt(jnp.mean(oh * oh, axis=1, keepdims=True) + EPS) * g
            o_ref[:, h * LANE:(h + 1) * LANE] = oh


def _paged_attention(q, knew, vnew, cache_k, cache_v, page_table, lam_row, g_row, n_heads, lam_init, pages):
    bd, n_tok, d = q.shape
    page = cache_k.shape[1] // n_heads
    n_pages = page_table.shape[1]
    rows = 2 * n_heads * n_tok

    def page_spec(p):
        return pl.BlockSpec((None, page * n_heads, LANE),
                            lambda b, j, pt: (pt[b * n_pages + j * pages + p], 0, 0))

    bspec = pl.BlockSpec((None, n_tok, d), lambda b, j, pt: (b, 0, 0))
    nspec = pl.BlockSpec((None, knew.shape[1], d), lambda b, j, pt: (b, 0, 0))
    rspec = pl.BlockSpec((1, LANE), lambda b, j, pt: (0, 0))
    grid_spec = pltpu.PrefetchScalarGridSpec(
        num_scalar_prefetch=1,
        grid=(bd, n_pages // pages),
        in_specs=[bspec] + [page_spec(p) for p in range(pages)] * 2 + [nspec, nspec, rspec, rspec],
        out_specs=bspec,
        scratch_shapes=[pltpu.VMEM((rows, d), BF16),
                        pltpu.VMEM((pages * page, d), BF16),
                        pltpu.VMEM((pages * page, d), BF16),
                        pltpu.VMEM((rows, LANE), F32),
                        pltpu.VMEM((rows, LANE), F32),
                        pltpu.VMEM((rows, d), F32)],
    )
    args = [page_table.reshape(-1), q] + [cache_k] * pages + [cache_v] * pages + [knew, vnew, lam_row, g_row]
    return pl.pallas_call(
        functools.partial(_paged_kernel, n_heads=n_heads, n_tok=n_tok, pages=pages, page=page,
                          lam_init=lam_init),
        out_shape=jax.ShapeDtypeStruct((bd, n_tok, d), F32),
        grid_spec=grid_spec,
        compiler_params=_cparams("parallel", "arbitrary"),
        name="paged_attention",
    )(*args)


def _mlstm_kernel(qk_ref, v_ref, om_ref, gcol_ref, grow_ref, cinit_ref, c0_ref, n0_ref, m0_ref,
                  cw_ref, cb_ref, gn_ref, y_ref, c_ref, n_ref, m_ref, xpad_ref, *, n_heads):
    ci = pl.program_id(1)
    t = qk_ref.shape[0]
    dm = v_ref.shape[1]
    dh = dm // n_heads
    hist = CONV_W - 1

    @pl.when(ci == 0)
    def _init():
        xpad_ref[0:8, :] = cinit_ref[...]
        c_ref[...] = c0_ref[...]
        n_ref[...] = n0_ref[...]
        m_ref[...] = m0_ref[...]

    x = qk_ref[...]
    xpad_ref[8:8 + t, :] = x
    y = cb_ref[...] + cw_ref[hist:hist + 1, :] * x
    for j in range(hist):
        y = y + cw_ref[j:j + 1, :] * xpad_ref[8 - hist + j:8 - hist + j + t, :]
    xpad_ref[0:8, :] = x[t - 8:t, :]
    y = y * _sigmoid(y)
    q_all = y[:, :dm].astype(BF16)
    k_all = (y[:, dm:] * (dh ** -0.5)).astype(BF16)

    gcol = gcol_ref[...]
    grow = grow_ref[...]
    r = lax.broadcasted_iota(jnp.int32, (t, t), 0)
    c = lax.broadcasted_iota(jnp.int32, (t, t), 1)
    causal = c <= r
    tril = jnp.where(causal, 1.0, 0.0).astype(F32)
    triu = jnp.where(r <= c, 1.0, 0.0).astype(F32)
    bc_all = jnp.dot(tril, gcol, preferred_element_type=F32, precision=lax.Precision.HIGHEST)
    br_all = jnp.dot(grow, triu, preferred_element_type=F32, precision=lax.Precision.HIGHEST)

    for h in range(n_heads):
        sl = slice(h * dh, (h + 1) * dh)
        qh, kh, vh = q_all[:, sl], k_all[:, sl], v_ref[:, sl]
        m_prev = m_ref[h][:, 0:1]
        b_c = bc_all[:, n_heads + h:n_heads + h + 1]
        li_c = gcol[:, h:h + 1]
        b_r = br_all[n_heads + h:n_heads + h + 1, :]
        li_r = grow[h:h + 1, :]
        log_d = jnp.where(causal, b_c - b_r + li_r, NEG)
        m_t = jnp.maximum(b_c + m_prev, jnp.max(log_d, axis=1, keepdims=True))
        s = _dot_nt(qh, kh) * jnp.exp(log_d - m_t)
        w_inter = jnp.exp(b_c + m_prev - m_t)
        ch = c_ref[h]
        num = _dot(s.astype(BF16), vh) + w_inter * _dot_nt(qh, ch.astype(BF16))
        nh = n_ref[h]
        den = jnp.sum(s, axis=1, keepdims=True) + w_inter * jnp.sum(qh.astype(F32) * nh, axis=1, keepdims=True)
        hout = num / jnp.maximum(jnp.abs(den), jnp.exp(-m_t))
        hn = hout * lax.rsqrt(jnp.mean(hout * hout, axis=1, keepdims=True) + EPS) * gn_ref[:, sl]
        y_ref[:, sl] = (_sigmoid(om_ref[:, sl].astype(F32)) * hn).astype(BF16)
        b_last = b_r[:, t - 1:t]
        log_w_r = b_last - b_r + li_r
        m_new = jnp.maximum(b_last + m_prev, jnp.max(log_w_r, axis=1, keepdims=True))
        w_c = jnp.exp(b_last - b_c + li_c - m_new)
        decay = jnp.exp(b_last + m_prev - m_new)
        wv = (w_c * vh.astype(F32)).astype(BF16)
        c_ref[h] = decay * ch + _dot_tn(wv, kh)
        n_ref[h] = decay * nh + jnp.sum(w_c * kh.astype(F32), axis=0, keepdims=True)
        m_ref[h] = jnp.broadcast_to(m_new, (1, LANE))


def _mlstm(qk, v, om, gcol, grow, conv_init, c0, n0, m0, conv_w, conv_b, g_norm, chunk, shared_init):
    bs, seq, dm2 = qk.shape
    dm = dm2 // 2
    n_heads, dh = c0.shape[1], c0.shape[2]
    ng = gcol.shape[2]
    nc = seq // chunk
    sidx = (lambda b: 0) if shared_init else (lambda b: b)
    row = lambda w: pl.BlockSpec((None, chunk, w), lambda b, i: (b, i, 0))
    return pl.pallas_call(
        functools.partial(_mlstm_kernel, n_heads=n_heads),
        out_shape=(jax.ShapeDtypeStruct((bs, seq, dm), BF16),
                   jax.ShapeDtypeStruct((bs, n_heads, dh, dh), F32),
                   jax.ShapeDtypeStruct((bs, n_heads, 1, dh), F32),
                   jax.ShapeDtypeStruct((bs, n_heads, 1, LANE), F32)),
        grid=(bs, nc),
        in_specs=[row(dm2), row(dm), row(dm), row(ng),
                  pl.BlockSpec((None, ng, chunk), lambda b, i: (b, 0, i)),
                  pl.BlockSpec((None, 8, dm2), lambda b, i: (sidx(b), 0, 0)),
                  pl.BlockSpec((None, n_heads, dh, dh), lambda b, i: (sidx(b), 0, 0, 0)),
                  pl.BlockSpec((None, n_heads, 1, dh), lambda b, i: (sidx(b), 0, 0, 0)),
                  pl.BlockSpec((None, n_heads, 1, LANE), lambda b, i: (sidx(b), 0, 0, 0)),
                  pl.BlockSpec((CONV_W, dm2), lambda b, i: (0, 0)),
                  pl.BlockSpec((1, dm2), lambda b, i: (0, 0)),
                  pl.BlockSpec((1, dm), lambda b, i: (0, 0))],
        out_specs=(row(dm),
                   pl.BlockSpec((None, n_heads, dh, dh), lambda b, i: (b, 0, 0, 0)),
                   pl.BlockSpec((None, n_heads, 1, dh), lambda b, i: (b, 0, 0, 0)),
                   pl.BlockSpec((None, n_heads, 1, LANE), lambda b, i: (b, 0, 0, 0))),
        scratch_shapes=[pltpu.VMEM((chunk + 8, dm2), F32)],
        compiler_params=_cparams("parallel", "arbitrary"),
        name="mlstm",
    )(qk, v, om, gcol, grow, conv_init, c0, n0, m0, conv_w, conv_b, g_norm)


def _layer_norm(z, g, b):
    mu = jnp.mean(z, axis=1, keepdims=True)
    zc = z - mu
    var = jnp.mean(zc * zc, axis=1, keepdims=True)
    return zc * lax.rsqrt(var + EPS) * g + b


def _merge_kernel(x_ref, ya_ref, ym_ref, ga_ref, gm_ref, wba_ref, wbm_ref, wout_ref, g_ref, b_ref,
                  wrh_ref, wrl_ref, h_ref, hrow_ref, s_ref, *, head_major):
    tm, d = x_ref.shape
    if head_major:
        ya = jnp.concatenate([ya_ref[h] for h in range(ya_ref.shape[0])], axis=1)
    else:
        ya = ya_ref[...]
    a = _dot(ya, wba_ref[...])
    m = _dot(ym_ref[...], wbm_ref[...])
    mix = ga_ref[...].astype(F32) * a + gm_ref[...].astype(F32) * m
    z = ALPHA * x_ref[...] + _dot(mix.astype(BF16), wout_ref[...])
    h = _layer_norm(z, g_ref[...], b_ref[...])
    h_ref[...] = h
    n_chunk = d // LANE
    for c in range(n_chunk):
        hrow_ref[pl.ds(c, tm, stride=n_chunk), :] = h[:, c * LANE:(c + 1) * LANE]
    hh = h.astype(BF16)
    hl = (h - hh.astype(F32)).astype(BF16)
    wrh = wrh_ref[...]
    s_ref[...] = _sigmoid(_dot(hh, wrh) + _dot(hl, wrh) + _dot(hh, wrl_ref[...]))


def _merge(x, ya, ym, ga, gm, wba, wbm, wout, g, b, wrh, wrl, tm, head_major):
    m, d = x.shape
    ne = wrh.shape[1]
    row = pl.BlockSpec((tm, d), lambda i: (i, 0))
    full = lambda a: pl.BlockSpec(a.shape, lambda i: (0,) * a.ndim)
    ya_spec = pl.BlockSpec((ya.shape[0], tm, LANE), lambda i: (0, i, 0)) if head_major else row
    return pl.pallas_call(
        functools.partial(_merge_kernel, head_major=head_major),
        out_shape=(jax.ShapeDtypeStruct((m, d), F32), jax.ShapeDtypeStruct((m * (d // LANE), LANE), F32),
                   jax.ShapeDtypeStruct((m, ne), F32)),
        grid=(m // tm,),
        in_specs=[row, ya_spec, row, row, row, full(wba), full(wbm), full(wout), full(g), full(b),
                  full(wrh), full(wrl)],
        out_specs=(row, pl.BlockSpec((tm * (d // LANE), LANE), lambda i: (i, 0)),
                   pl.BlockSpec((tm, ne), lambda i: (i, 0))),
        compiler_params=_cparams("parallel"),
        name="merge_ln_router",
    )(x, ya, ym, ga, gm, wba, wbm, wout, g, b, wrh, wrl)


def _route_kernel(s_ref, b_ref, idx_ref, rank_ref, gate_ref, cnt_ref, carry_ref, *, n_exp):
    i = pl.program_id(0)

    @pl.when(i == 0)
    def _init():
        carry_ref[...] = jnp.zeros(carry_ref.shape, F32)

    s = s_ref[...]
    tm = s.shape[0]
    ssel = s + b_ref[...]
    gsz = n_exp // N_GROUPS
    ninf = -jnp.inf
    lane_i = lax.broadcasted_iota(jnp.int32, (tm, n_exp), 1)
    lane = lane_i.astype(F32)
    lgrp = lane_i // gsz
    slot = lax.broadcasted_iota(jnp.int32, (tm, LANE), 1)

    grp = []
    for g in range(N_GROUPS):
        mg = jnp.where(lgrp == g, ssel, ninf)
        m1 = jnp.max(mg, axis=1, keepdims=True)
        i1 = jnp.min(jnp.where(mg == m1, lane, float(n_exp)), axis=1, keepdims=True)
        m2 = jnp.max(jnp.where(lane == i1, ninf, mg), axis=1, keepdims=True)
        grp.append(m1 + m2)
    enabled = jnp.zeros((tm, n_exp), F32)
    for g in range(N_GROUPS):
        ahead = jnp.zeros((tm, 1), F32)
        for j in range(N_GROUPS):
            if j != g:
                beats = (grp[j] > grp[g]) if j > g else (grp[j] >= grp[g])
                ahead = ahead + jnp.where(beats, 1.0, 0.0)
        keep = jnp.where(ahead < TOPK_GROUPS, 1.0, 0.0)
        enabled = jnp.where(lgrp == g, keep, enabled)
    cur = jnp.where(enabled > 0.5, ssel, ninf)

    sel = jnp.zeros((tm, n_exp), F32)
    idx_out = jnp.zeros((tm, LANE), F32)
    gate_out = jnp.zeros((tm, LANE), F32)
    gsum = jnp.zeros((tm, 1), F32)
    picks = []
    for k in range(TOP_K):
        m = jnp.max(cur, axis=1, keepdims=True)
        ik = jnp.min(jnp.where(cur == m, lane, float(n_exp)), axis=1, keepdims=True)
        pick = lane == ik
        gk = jnp.sum(jnp.where(pick, s, 0.0), axis=1, keepdims=True)
        cur = jnp.where(pick, ninf, cur)
        sel = jnp.where(pick, 1.0, sel)
        idx_out = jnp.where(slot == k, ik, idx_out)
        gate_out = jnp.where(slot == k, gk, gate_out)
        gsum = gsum + gk
        picks.append(ik)
    idx_ref[...] = idx_out
    gate_ref[...] = gate_out / gsum * ROUTE_SCALE

    r = lax.broadcasted_iota(jnp.int32, (tm, tm), 0)
    c = lax.broadcasted_iota(jnp.int32, (tm, tm), 1)
    ltri = jnp.where(c < r, 1.0, 0.0).astype(BF16)
    rank_dense = carry_ref[...] + _dot(ltri, sel.astype(BF16))
    rank_out = jnp.zeros((tm, LANE), F32)
    for k in range(TOP_K):
        rk = jnp.sum(jnp.where(lane == picks[k], rank_dense, 0.0), axis=1, keepdims=True)
        rank_out = jnp.where(slot == k, rk, rank_out)
    rank_ref[...] = rank_out
    total = carry_ref[...] + jnp.sum(sel, axis=0, keepdims=True)
    carry_ref[...] = total
    cnt_ref[...] = jnp.broadcast_to(total, cnt_ref.shape)


def _route(s, b_row, tm):
    t, n_exp = s.shape
    tok = jax.ShapeDtypeStruct((t, LANE), F32)
    tspec = pl.BlockSpec((tm, LANE), lambda i: (i, 0))
    return pl.pallas_call(
        functools.partial(_route_kernel, n_exp=n_exp),
        out_shape=(tok, tok, tok, jax.ShapeDtypeStruct((8, n_exp), F32)),
        grid=(t // tm,),
        in_specs=[pl.BlockSpec((tm, n_exp), lambda i: (i, 0)), pl.BlockSpec((1, n_exp), lambda i: (0, 0))],
        out_specs=(tspec, tspec, tspec, pl.BlockSpec((8, n_exp), lambda i: (0, 0))),
        scratch_shapes=[pltpu.VMEM((1, n_exp), F32)],
        compiler_params=_cparams("arbitrary"),
        name="route",
    )(s, b_row)


def _row_copy(src_ref, s0, dst_ref, d0, rows, sem):
    return pltpu.make_async_copy(src_ref.at[pl.ds(pl.multiple_of(s0, 8), rows)],
                                 dst_ref.at[pl.ds(pl.multiple_of(d0, 8), rows)], sem)


RANK_BITS = 20


def _sorted_row(starts_ref, code, rows):
    return (starts_ref[code >> RANK_BITS] + (code & ((1 << RANK_BITS) - 1))) * rows


def _scatter_kernel(starts_ref, code_ref, src_ref, dst_ref, sem, *, n_tok, fan, rows):
    def issue(t, carry):
        for u in range(fan):
            d0 = _sorted_row(starts_ref, code_ref[0, t * fan + u], rows)
            _row_copy(src_ref, t * rows, dst_ref, d0, rows, sem).start()
        return carry
    lax.fori_loop(0, n_tok, issue, 0)

    def drain(t, carry):
        for u in range(fan):
            _row_copy(src_ref, 0, dst_ref, 0, rows, sem).wait()
        return carry
    lax.fori_loop(0, n_tok, drain, 0)


def _scatter_rows(src, starts, code, n_dst_rows, n_tok, fan, rows):
    steps = src.shape[0] // (n_tok * rows)
    grid_spec = pltpu.PrefetchScalarGridSpec(
        num_scalar_prefetch=1,
        grid=(steps,),
        in_specs=[pl.BlockSpec((None, 1, n_tok * fan), lambda i, st: (i, 0, 0), memory_space=pltpu.SMEM),
                  pl.BlockSpec((n_tok * rows, src.shape[1]), lambda i, st: (i, 0))],
        out_specs=pl.BlockSpec(memory_space=pl.ANY),
        scratch_shapes=[pltpu.SemaphoreType.DMA(())],
    )
    return pl.pallas_call(
        functools.partial(_scatter_kernel, n_tok=n_tok, fan=fan, rows=rows),
        out_shape=jax.ShapeDtypeStruct((n_dst_rows, src.shape[1]), src.dtype),
        grid_spec=grid_spec,
        compiler_params=_cparams("arbitrary"),
        name="scatter_rows",
    )(starts, code.reshape(steps, 1, n_tok * fan), src)


def _expert_kernel(win_ref, exp_ref, lo_ref, hi_ref, x_ref, w1_ref, w3_ref, w2_ref, y_ref,
                   w1b_ref, w3b_ref, w2b_ref, xb_ref, *, blk, n_chunk):
    s = pl.program_id(0)
    sp = jnp.maximum(s - 1, 0)
    new_win = jnp.logical_or(s == 0, win_ref[s] != win_ref[sp])
    new_exp = jnp.logical_or(s == 0, exp_ref[s] != exp_ref[sp])
    lo = lo_ref[s]
    hi = hi_ref[s]

    @pl.when(new_win)
    def _load_rows():
        for c in range(n_chunk):
            xb_ref[:, c * LANE:(c + 1) * LANE] = x_ref[pl.ds(c, blk, stride=n_chunk), :].astype(BF16)

    @pl.when(new_exp)
    def _cast():
        w1b_ref[...] = w1_ref[...].astype(BF16)
        w3b_ref[...] = w3_ref[...].astype(BF16)
        w2b_ref[...] = w2_ref[...].astype(BF16)

    def run(first_visit):
        x = xb_ref[...]
        a = _dot(x, w1b_ref[...])
        g = (a * _sigmoid(a)) * _dot(x, w3b_ref[...])
        y = _dot(g.astype(BF16), w2b_ref[...])
        row = lax.broadcasted_iota(jnp.int32, (blk, LANE), 0)
        mine = jnp.logical_and(row >= lo, row < hi)
        for c in range(n_chunk):
            sl = pl.ds(c, blk, stride=n_chunk)
            other = 0.0 if first_visit else y_ref[sl, :]
            y_ref[sl, :] = jnp.where(mine, y[:, c * LANE:(c + 1) * LANE], other)

    @pl.when(jnp.logical_and(hi > lo, new_win))
    def _run_first():
        run(True)

    @pl.when(jnp.logical_and(hi > lo, jnp.logical_not(new_win)))
    def _run_again():
        run(False)


def _experts(xrow, win, exp, lo, hi, w1, w3, w2, blk):
    d, de = w1.shape[1], w1.shape[2]
    n_chunk = d // LANE
    n_steps = win.shape[0]
    xspec = pl.BlockSpec((blk * n_chunk, LANE), lambda s, wn, ex, l, h: (wn[s], 0))
    grid_spec = pltpu.PrefetchScalarGridSpec(
        num_scalar_prefetch=4,
        grid=(n_steps,),
        in_specs=[xspec,
                  pl.BlockSpec((None, d, de), lambda s, wn, ex, l, h: (ex[s], 0, 0)),
                  pl.BlockSpec((None, d, de), lambda s, wn, ex, l, h: (ex[s], 0, 0)),
                  pl.BlockSpec((None, de, d), lambda s, wn, ex, l, h: (ex[s], 0, 0))],
        out_specs=xspec,
        scratch_shapes=[pltpu.VMEM((d, de), BF16), pltpu.VMEM((d, de), BF16), pltpu.VMEM((de, d), BF16),
                        pltpu.VMEM((blk, d), BF16)],
    )
    return pl.pallas_call(
        functools.partial(_expert_kernel, blk=blk, n_chunk=n_chunk),
        out_shape=jax.ShapeDtypeStruct(xrow.shape, F32),
        grid_spec=grid_spec,
        compiler_params=_cparams("arbitrary"),
        name="experts",
    )(win, exp, lo, hi, xrow, w1, w3, w2)


def _final_kernel(starts_ref, code_ref, code_next_ref, h_ref, gate_ref, yrow_ref, ws1_ref, ws3_ref, ws2_ref,
                  g_ref, b_ref, o_ref, ybuf_ref, sem, *, n_chunk):
    i = pl.program_id(0)
    slot = i % 2
    h = h_ref[...]
    tm = h.shape[0]

    def gather(cref, s):
        def issue(t, carry):
            for k in range(TOP_K):
                s0 = _sorted_row(starts_ref, cref[0, t * TOP_K + k], n_chunk)
                _row_copy(yrow_ref, s0, ybuf_ref.at[s], (k * tm + t) * n_chunk, n_chunk, sem.at[s]).start()
            return carry
        lax.fori_loop(0, tm, issue, 0)

    @pl.when(i == 0)
    def _prime():
        gather(code_ref, 0)

    @pl.when(i + 1 < pl.num_programs(0))
    def _prefetch():
        gather(code_next_ref, 1 - slot)

    yg_ref = ybuf_ref.at[slot]

    def drain(t, carry):
        for k in range(TOP_K):
            _row_copy(yrow_ref, 0, yg_ref, 0, n_chunk, sem.at[slot]).wait()
        return carry
    lax.fori_loop(0, tm, drain, 0)

    x = h.astype(BF16)
    a = _dot(x, ws1_ref[...])
    gg = (a * _sigmoid(a)) * _dot(x, ws3_ref[...])
    shared = _dot(gg.astype(BF16), ws2_ref[...])
    gate = gate_ref[...]
    gk = [jnp.broadcast_to(gate[:, k:k + 1], (tm, LANE)) for k in range(TOP_K)]
    chunks = []
    for c in range(n_chunk):
        acc = gk[0] * yg_ref[pl.ds(c, tm, stride=n_chunk), :]
        for k in range(1, TOP_K):
            acc = acc + gk[k] * yg_ref[pl.ds(k * tm * n_chunk + c, tm, stride=n_chunk), :]
        chunks.append(acc)
    f = jnp.concatenate(chunks, axis=1) + shared
    o_ref[...] = _layer_norm(ALPHA * h + f, g_ref[...], b_ref[...])


def _final(h, yrow, starts, code, gate, ws1, ws3, ws2, g, b, tm, blk_off):
    m, d = h.shape
    n_chunk = d // LANE
    n_blk = m // tm
    row = pl.BlockSpec((tm, d), lambda i, st: (i, 0))
    full = lambda a: pl.BlockSpec(a.shape, lambda i, st: (0,) * a.ndim)
    cspec = lambda nxt: pl.BlockSpec((None, 1, tm * TOP_K),
                                     lambda i, st: (jnp.minimum(i + nxt, n_blk - 1) + blk_off, 0, 0),
                                     memory_space=pltpu.SMEM)
    grid_spec = pltpu.PrefetchScalarGridSpec(
        num_scalar_prefetch=1,
        grid=(n_blk,),
        in_specs=[cspec(0), cspec(1), row, pl.BlockSpec((tm, LANE), lambda i, st: (i + blk_off, 0)),
                  pl.BlockSpec(memory_space=pl.ANY),
                  full(ws1), full(ws3), full(ws2), full(g), full(b)],
        out_specs=row,
        scratch_shapes=[pltpu.VMEM((2, tm * TOP_K * n_chunk, LANE), F32), pltpu.SemaphoreType.DMA((2,))],
    )
    code3 = code.reshape(-1, 1, tm * TOP_K)
    return pl.pallas_call(
        functools.partial(_final_kernel, n_chunk=n_chunk),
        out_shape=jax.ShapeDtypeStruct((m, d), F32),
        grid_spec=grid_spec,
        compiler_params=_cparams("arbitrary"),
        name="shared_ln",
    )(starts, code3, code3, h, gate, yrow, ws1, ws3, ws2, g, b)


def _rope_tables(pos):
    half = ROPE_DIM // 2
    inv = ROPE_THETA ** (-jnp.arange(half, dtype=F32) * 2.0 / ROPE_DIM)
    ang = pos.astype(F32)[:, None] * inv
    cos8, sin8 = jnp.cos(ang), jnp.sin(ang)
    n = pos.shape[0]
    cos = jnp.concatenate([cos8, cos8, jnp.ones((n, D_QK - ROPE_DIM), F32)], axis=1)
    sin = jnp.concatenate([-sin8, sin8, jnp.zeros((n, D_QK - ROPE_DIM), F32)], axis=1)
    return jnp.tile(cos, (1, LANE // D_QK)), jnp.tile(sin, (1, LANE // D_QK))


def _expert_steps(starts, ends, n_win, blk):
    n_exp = starts.shape[0]
    first_w = starts // blk
    n_w = jnp.where(ends > starts, (ends + blk - 1) // blk - first_w, 0)
    step_end = jnp.cumsum(n_w)
    step_start = step_end - n_w
    total = step_end[-1]
    s = jnp.arange(n_win + n_exp, dtype=jnp.int32)
    e = jnp.minimum(jnp.searchsorted(step_end, s, side='right'), n_exp - 1).astype(jnp.int32)
    w = first_w[e] + (s - step_start[e])
    lo = jnp.maximum(starts[e], w * blk) - w * blk
    hi = jnp.minimum(ends[e], (w + 1) * blk) - w * blk
    valid = s < total
    last = jnp.maximum(total - 1, 0)
    e = jnp.where(valid, e, e[last])
    w = jnp.where(valid, w, w[last])
    lo = jnp.where(valid, lo, 0)
    hi = jnp.where(valid, hi, 0)
    return w.astype(jnp.int32), e, lo.astype(jnp.int32), hi.astype(jnp.int32)


def _row_tile(m, pref):
    t = pref
    while m % t:
        t //= 2
    return t


def kernel(x_prompt, x_sample, cache_k, cache_v, page_table, state_C, state_n, state_m, state_conv, meta_tokens, w_in, b_in, lam_q1, lam_k1, lam_q2, lam_k2, g_subln, conv_w, conv_b, g_mnorm, w_ba, w_bm, w_out, ln1_g, ln1_b, w_router, b_router, w1, w3, w2, ws1, ws3, ws2, ln2_g, ln2_b):
    B, SEQ, D = x_prompt.shape
    BD, S, _ = x_sample.shape
    NHA, DVA = cache_k.shape[3], cache_v.shape[4]
    NHM, DHM = state_C.shape[2], state_C.shape[3]
    n_exp = w_router.shape[2]
    n_pages = page_table.shape[1]
    page = cache_k.shape[2]
    l = 0
    lam_init = 0.8 - 0.6 * math.exp(-0.3 * l)
    lam = (jnp.exp(jnp.sum(lam_q1[l] * lam_k1[l])) - jnp.exp(jnp.sum(lam_q2[l] * lam_k2[l])) + lam_init)
    lam_row = jnp.broadcast_to(lam.astype(F32), (1, LANE))
    g_row = g_subln[l].astype(F32).reshape(1, LANE)

    w = w_in[l]
    bias = b_in[l].astype(F32)
    o_mix = 3 * D
    o_if = o_mix + 4 * D
    o_g = o_if + 2 * NHM
    w_qkv = w[:, :o_mix].astype(BF16)
    w_mix = w[:, o_mix:o_if].astype(BF16)
    w_g = w[:, o_g:].astype(BF16)
    w_if = w[:, o_if:o_g]
    b_qkv = bias[None, :o_mix]
    b_mix = bias[None, o_mix:o_if]
    b_g = bias[None, o_g:]
    b_if = bias[o_if:o_g]

    MP = B * SEQ
    xp = x_prompt.reshape(MP, D)
    xs = x_sample.reshape(BD * S, D)
    xm = jnp.zeros((PAD_CHUNK, D), F32).at[:N_META].set(meta_tokens.astype(F32))
    tmp = _row_tile(SEQ, 512)
    tms = _row_tile(BD * S, 256)

    cos_p, sin_p = _rope_tables(N_META + jnp.arange(SEQ))
    cos_s, sin_s = _rope_tables(jnp.tile(n_pages * page + jnp.arange(S), BD))
    cos_m, sin_m = _rope_tables(jnp.arange(PAD_CHUNK))

    xp_bf, xs_bf, xm_bf = xp.astype(BF16), xs.astype(BF16), xm.astype(BF16)
    qlo_p, qhi_p, kf_p, kb_p, vf_p, vb_p = _proj_qkv(xp_bf, w_qkv, b_qkv, cos_p, sin_p, tmp, NHA)
    qlo_s, qhi_s, kf_s, _, vf_s, _ = _proj_qkv(xs_bf, w_qkv, b_qkv, cos_s, sin_s, tms, NHA)
    _, _, kf_m, kb_m, vf_m, vb_m = _proj_qkv(xm_bf, w_qkv, b_qkv, cos_m, sin_m, PAD_CHUNK, NHA)

    qk_p, vm_p, om_p = _proj_mix(xp_bf, w_mix, b_mix, tmp)
    qk_s, vm_s, om_s = _proj_mix(xs_bf, w_mix, b_mix, tms)
    qk_m, vm_m, om_m = _proj_mix(xm_bf, w_mix, b_mix, PAD_CHUNK)

    ga_p, gm_p = _proj_gates(xp_bf, w_g, b_g, tmp)
    ga_s, gm_s = _proj_gates(xs_bf, w_g, b_g, tms)

    if_p = _proj_if(xp, w_if, b_if, tmp, NHM)[:, :2 * NHM]
    if_s = _proj_if(xs, w_if, b_if, tms, NHM)[:, :2 * NHM]
    if_m = _proj_if(xm, w_if, b_if, PAD_CHUNK, NHM)[:, :2 * NHM]

    ya_p = _prompt_attention(qlo_p, qhi_p, kb_p, vb_p, kb_m, vb_m, lam_row, g_row, B, lam_init,
                             _row_tile(SEQ, ATTN_TILE))
    q_s = jnp.transpose(qlo_s + qhi_s, (1, 0, 2)).reshape(BD, S, D).astype(F32)
    kn = jnp.zeros((BD, LANE, D), BF16).at[:, :S].set(kf_s.reshape(BD, S, D).astype(BF16))
    vn = jnp.zeros((BD, LANE, D), BF16).at[:, :S].set(vf_s.reshape(BD, S, D).astype(BF16))
    n_pool = cache_k.shape[1]
    ya_s = _paged_attention(q_s, kn, vn, cache_k.reshape(-1, page * NHA, DVA), cache_v.reshape(-1, page * NHA, DVA),
                            page_table + l * n_pool, lam_row, g_row, NHA, lam_init,
                            math.gcd(PAGES_PER_STEP, n_pages))

    cw = conv_w[l].astype(F32)
    cb = conv_b[l].astype(F32)[None]
    gn = g_mnorm[l].astype(F32)[None]
    pad_gate = jnp.concatenate([jnp.full((NHM,), NEG, F32), jnp.zeros((NHM,), F32)])

    def gate_layouts(g, bs, real, total):
        g = g.reshape(bs, -1, 2 * NHM)[:, :real]
        if total > real:
            g = jnp.concatenate([g, jnp.broadcast_to(pad_gate, (bs, total - real, 2 * NHM))], axis=1)
        return g, jnp.transpose(g, (0, 2, 1))

    def pad_rows(a, bs, real, total):
        a = a.reshape(bs, -1, a.shape[-1])[:, :real]
        return jnp.pad(a, ((0, 0), (0, total - real), (0, 0)))

    gcol_m, grow_m = gate_layouts(if_m, 1, N_META, PAD_CHUNK)
    zC = jnp.zeros((1, NHM, DHM, DHM), F32)
    zn = jnp.zeros((1, NHM, 1, DHM), F32)
    zm = jnp.zeros((1, NHM, 1, LANE), F32)
    _, c_m, n_m, m_m = _mlstm(pad_rows(qk_m, 1, N_META, PAD_CHUNK), pad_rows(vm_m, 1, N_META, PAD_CHUNK),
                              pad_rows(om_m, 1, N_META, PAD_CHUNK), gcol_m, grow_m,
                              jnp.zeros((1, 8, 2 * D), F32), zC, zn, zm, cw, cb, gn, PAD_CHUNK, True)
    hist_m = jnp.zeros((1, 8, 2 * D), F32).at[0, 8 - (CONV_W - 1):].set(qk_m[N_META - (CONV_W - 1):N_META])
    gcol_p, grow_p = gate_layouts(if_p, B, SEQ, SEQ)
    ym_p, c_p, n_p, m_p = _mlstm(qk_p.reshape(B, SEQ, 2 * D), vm_p.reshape(B, SEQ, D), om_p.reshape(B, SEQ, D),
                                 gcol_p, grow_p, hist_m, c_m, n_m, m_m, cw, cb, gn,
                                 _row_tile(SEQ, MLSTM_CHUNK), True)
    gcol_s, grow_s = gate_layouts(if_s, BD, S, PAD_CHUNK)
    hist_s = jnp.zeros((BD, 8, 2 * D), F32).at[:, 8 - (CONV_W - 1):].set(state_conv[l].astype(F32))
    ym_s, c_s, n_s, m_s = _mlstm(pad_rows(qk_s, BD, S, PAD_CHUNK), pad_rows(vm_s, BD, S, PAD_CHUNK),
                                 pad_rows(om_s, BD, S, PAD_CHUNK), gcol_s, grow_s, hist_s,
                                 state_C[l].astype(F32), state_n[l].astype(F32)[:, :, None, :],
                                 jnp.broadcast_to(state_m[l].astype(F32)[:, :, None, None], (BD, NHM, 1, LANE)),
                                 cw, cb, gn, PAD_CHUNK, False)
    ym_s = ym_s[:, :S].reshape(BD * S, D)

    wba, wbm, wout = w_ba[l].astype(BF16), w_bm[l].astype(BF16), w_out[l].astype(BF16)
    g1, b1 = ln1_g[l].astype(F32)[None], ln1_b[l].astype(F32)[None]
    wr = w_router[l].astype(F32)
    wrh = wr.astype(BF16)
    wrl = (wr - wrh.astype(F32)).astype(BF16)
    h_p, hrow_p, s_p = _merge(xp, ya_p, ym_p.reshape(MP, D), ga_p, gm_p, wba, wbm, wout, g1, b1, wrh, wrl,
                              _row_tile(MP, 256), True)
    h_s, hrow_s, s_s = _merge(xs, ya_s.reshape(BD * S, D).astype(BF16), ym_s, ga_s, gm_s, wba, wbm, wout,
                              g1, b1, wrh, wrl, tms, False)

    MT = MP + BD * S
    n_chunk = D // LANE
    tk = MT * TOP_K
    tt = _row_tile(math.gcd(MP, BD * S), 256)
    hrow = jnp.concatenate([hrow_p, hrow_s], axis=0)
    s_all = jnp.concatenate([s_p, s_s], axis=0)
    idx_f, rank_f, gate, cnt = _route(s_all, b_router[l].astype(F32)[None], tt)
    idx = idx_f[:, :TOP_K].astype(jnp.int32)
    counts = cnt[0].astype(jnp.int32)
    ends = jnp.cumsum(counts)
    starts = ends - counts
    assert MT < (1 << RANK_BITS)
    code = ((idx << RANK_BITS) | rank_f[:, :TOP_K].astype(jnp.int32)).reshape(tk)
    n_win = -(-tk // MOE_BLOCK)
    xrow = _scatter_rows(hrow, starts, code, n_win * MOE_BLOCK * n_chunk, tt, TOP_K, n_chunk)
    win, exp, lo, hi = _expert_steps(starts, ends, n_win, MOE_BLOCK)
    yrow = _experts(xrow, win, exp, lo, hi, w1[l], w3[l], w2[l], MOE_BLOCK)

    ws1b, ws3b, ws2b = ws1[l].astype(BF16), ws3[l].astype(BF16), ws2[l].astype(BF16)
    g2, b2 = ln2_g[l].astype(F32)[None], ln2_b[l].astype(F32)[None]
    y_p = _final(h_p, yrow, starts, code, gate, ws1b, ws3b, ws2b, g2, b2, tt, 0)
    y_s = _final(h_s, yrow, starts, code, gate, ws1b, ws3b, ws2b, g2, b2, tt, MP // tt)

    def with_meta(meta_rows, real):
        meta = jnp.broadcast_to(meta_rows[None, :N_META], (B, N_META, D))
        return jnp.concatenate([meta, real.reshape(B, SEQ, D)], axis=1).reshape(1, B, N_META + SEQ, NHA, DVA)

    k_prompt = with_meta(kf_m, kf_p)
    v_prompt = with_meta(vf_m, vf_p)
    conv_prompt = qk_p.reshape(B, SEQ, 2 * D)[:, SEQ - (CONV_W - 1):][None]
    conv_sample = qk_s.reshape(BD, S, 2 * D)[:, S - (CONV_W - 1):][None]
    return (y_p.reshape(B, SEQ, D), y_s.reshape(BD, S, D),
            k_prompt, v_prompt,
            c_p[None], n_p[:, :, 0][None], m_p[:, :, 0, 0][None], conv_prompt,
            kf_s.reshape(1, BD, S, NHA, DVA), vf_s.reshape(1, BD, S, NHA, DVA),
            c_s[None], n_s[:, :, 0][None], m_s[:, :, 0, 0][None], conv_sample)
```

```python
import functools
import math

import jax
import jax.numpy as jnp
from jax import lax
from jax.experimental import pallas as pl
from jax.experimental.pallas import tpu as pltpu

F32 = jnp.float32
BF16 = jnp.bfloat16

DEPTH = 1
N_META = 16
D_QK = 64
ROPE_DIM = D_QK // 4
ROPE_THETA = 500000.0
CONV_W = 4
TOP_K = 8
N_GROUPS = 8
TOPK_GROUPS = 4
ROUTE_SCALE = 2.5
ALPHA = (2 * DEPTH) ** 0.25
EPS = 1e-5
NEG = -1e30

LANE = 128
VMEM_LIMIT = 52 * 1024 * 1024

ATTN_TILE = 512
MLSTM_CHUNK = 256
PAD_CHUNK = 128
PAGES_PER_STEP = 16
MOE_BLOCK = 512


def _cparams(*sem):
    return pltpu.CompilerParams(dimension_semantics=sem, vmem_limit_bytes=VMEM_LIMIT)


def _dot(a, b):
    return jnp.dot(a, b, preferred_element_type=F32)


def _dot_nt(a, b):
    return lax.dot_general(a, b, (((1,), (1,)), ((), ())), preferred_element_type=F32)


def _dot_tn(a, b):
    return lax.dot_general(a, b, (((0,), (0,)), ((), ())), preferred_element_type=F32)


def _split3(a):
    a1 = a.astype(BF16)
    r1 = a - a1.astype(F32)
    a2 = r1.astype(BF16)
    a3 = (r1 - a2.astype(F32)).astype(BF16)
    return a1, a2, a3


def _sigmoid(x):
    return 1.0 / (1.0 + jnp.exp(-x))


def _qkv_kernel(x_ref, w_ref, b_ref, cos_ref, sin_ref,
                qlo_ref, qhi_ref, kf_ref, kb_ref, vf_ref, vb_ref, *, n_heads, d_model):
    x = x_ref[...]
    cos = cos_ref[...]
    sin = sin_ref[...]
    lane = lax.broadcasted_iota(jnp.int32, cos.shape, 1)
    first = (lane % D_QK) < (ROPE_DIM // 2)
    lo_half = lane < D_QK
    half = ROPE_DIM // 2

    def rope(s):
        up = pltpu.roll(s, LANE - half, axis=1)
        dn = pltpu.roll(s, half, axis=1)
        return s * cos + jnp.where(first, up, dn) * sin

    zq = _dot(x, w_ref[:, 0:d_model]) + b_ref[:, 0:d_model]
    for h in range(n_heads):
        q = rope(zq[:, h * LANE:(h + 1) * LANE]) * (D_QK ** -0.5)
        qlo_ref[h] = jnp.where(lo_half, q, 0.0).astype(BF16)
        qhi_ref[h] = jnp.where(lo_half, 0.0, q).astype(BF16)
    zk = _dot(x, w_ref[:, d_model:2 * d_model]) + b_ref[:, d_model:2 * d_model]
    for h in range(n_heads):
        k = rope(zk[:, h * LANE:(h + 1) * LANE])
        kf_ref[:, h * LANE:(h + 1) * LANE] = k
        kb_ref[h] = k.astype(BF16)
    zv = _dot(x, w_ref[:, 2 * d_model:3 * d_model]) + b_ref[:, 2 * d_model:3 * d_model]
    vf_ref[...] = zv
    for h in range(n_heads):
        vb_ref[h] = zv[:, h * LANE:(h + 1) * LANE].astype(BF16)


def _proj_qkv(x_bf, w_bf, bias, cos, sin, tm, n_heads):
    m, d = x_bf.shape
    n_pos = cos.shape[0] // tm
    hm = jax.ShapeDtypeStruct((n_heads, m, LANE), BF16)
    fm = jax.ShapeDtypeStruct((m, d), F32)
    hspec = pl.BlockSpec((n_heads, tm, LANE), lambda i: (0, i, 0))
    fspec = pl.BlockSpec((tm, d), lambda i: (i, 0))
    pspec = pl.BlockSpec((tm, LANE), lambda i: (i % n_pos, 0))
    return pl.pallas_call(
        functools.partial(_qkv_kernel, n_heads=n_heads, d_model=d),
        out_shape=(hm, hm, fm, hm, fm, hm),
        grid=(m // tm,),
        in_specs=[fspec, pl.BlockSpec((d, 3 * d), lambda i: (0, 0)),
                  pl.BlockSpec((1, 3 * d), lambda i: (0, 0)), pspec, pspec],
        out_specs=(hspec, hspec, fspec, hspec, fspec, hspec),
        compiler_params=_cparams("parallel"),
        name="proj_qkv",
    )(x_bf, w_bf, bias, cos, sin)


def _mix_proj_kernel(x_ref, w_ref, b_ref, qk_ref, v_ref, om_ref, *, d_model):
    x = x_ref[...]
    d = d_model
    qk_ref[...] = _dot(x, w_ref[:, 0:2 * d]) + b_ref[:, 0:2 * d]
    v_ref[...] = (_dot(x, w_ref[:, 2 * d:3 * d]) + b_ref[:, 2 * d:3 * d]).astype(BF16)
    om_ref[...] = (_dot(x, w_ref[:, 3 * d:4 * d]) + b_ref[:, 3 * d:4 * d]).astype(BF16)


def _proj_mix(x_bf, w_bf, bias, tm):
    m, d = x_bf.shape
    return pl.pallas_call(
        functools.partial(_mix_proj_kernel, d_model=d),
        out_shape=(jax.ShapeDtypeStruct((m, 2 * d), F32), jax.ShapeDtypeStruct((m, d), BF16),
                   jax.ShapeDtypeStruct((m, d), BF16)),
        grid=(m // tm,),
        in_specs=[pl.BlockSpec((tm, d), lambda i: (i, 0)),
                  pl.BlockSpec((d, 4 * d), lambda i: (0, 0)),
                  pl.BlockSpec((1, 4 * d), lambda i: (0, 0))],
        out_specs=(pl.BlockSpec((tm, 2 * d), lambda i: (i, 0)), pl.BlockSpec((tm, d), lambda i: (i, 0)),
                   pl.BlockSpec((tm, d), lambda i: (i, 0))),
        compiler_params=_cparams("parallel"),
        name="proj_mix",
    )(x_bf, w_bf, bias)


def _gate_proj_kernel(x_ref, w_ref, b_ref, ga_ref, gm_ref, *, d_model):
    x = x_ref[...]
    d = d_model
    ga_ref[...] = _sigmoid(_dot(x, w_ref[:, 0:d]) + b_ref[:, 0:d]).astype(BF16)
    gm_ref[...] = _sigmoid(_dot(x, w_ref[:, d:2 * d]) + b_ref[:, d:2 * d]).astype(BF16)


def _proj_gates(x_bf, w_bf, bias, tm):
    m, d = x_bf.shape
    return pl.pallas_call(
        functools.partial(_gate_proj_kernel, d_model=d),
        out_shape=(jax.ShapeDtypeStruct((m, d), BF16), jax.ShapeDtypeStruct((m, d), BF16)),
        grid=(m // tm,),
        in_specs=[pl.BlockSpec((tm, d), lambda i: (i, 0)),
                  pl.BlockSpec((d, 2 * d), lambda i: (0, 0)),
                  pl.BlockSpec((1, 2 * d), lambda i: (0, 0))],
        out_specs=(pl.BlockSpec((tm, d), lambda i: (i, 0)), pl.BlockSpec((tm, d), lambda i: (i, 0))),
        compiler_params=_cparams("parallel"),
        name="proj_gates",
    )(x_bf, w_bf, bias)


def _if_proj_kernel(x_ref, wh_ref, wl_ref, b_ref, o_ref, *, n_heads):
    x = x_ref[...]
    xh = x.astype(BF16)
    xl = (x - xh.astype(F32)).astype(BF16)
    wh = wh_ref[...]
    z = _dot(xh, wh) + _dot(xl, wh) + _dot(xh, wl_ref[...]) + b_ref[...]
    lane = lax.broadcasted_iota(jnp.int32, z.shape, 1)
    logsig = jnp.minimum(z, 0.0) - jnp.log(1.0 + jnp.exp(-jnp.abs(z)))
    o_ref[...] = jnp.where(lane < n_heads, z, logsig)


def _proj_if(x_f32, w_if, b_if, tm, n_heads):
    m, d = x_f32.shape
    wpad = jnp.zeros((d, LANE), F32).at[:, :2 * n_heads].set(w_if)
    bpad = jnp.zeros((1, LANE), F32).at[0, :2 * n_heads].set(b_if)
    wh = wpad.astype(BF16)
    wl = (wpad - wh.astype(F32)).astype(BF16)
    return pl.pallas_call(
        functools.partial(_if_proj_kernel, n_heads=n_heads),
        out_shape=jax.ShapeDtypeStruct((m, LANE), F32),
        grid=(m // tm,),
        in_specs=[pl.BlockSpec((tm, d), lambda i: (i, 0)),
                  pl.BlockSpec((d, LANE), lambda i: (0, 0)),
                  pl.BlockSpec((d, LANE), lambda i: (0, 0)),
                  pl.BlockSpec((1, LANE), lambda i: (0, 0))],
        out_specs=pl.BlockSpec((tm, LANE), lambda i: (i, 0)),
        compiler_params=_cparams("parallel"),
        name="proj_if",
    )(x_f32, wh, wl, bpad)


def _flash_update(idx, q, kvs, m_ref, l_ref, acc_ref, first):
    ss = []
    for k, _, mask in kvs:
        s = _dot_nt(q, k)
        ss.append(s if mask is None else jnp.where(mask, s, NEG))
    m_cur = functools.reduce(jnp.maximum, [jnp.max(s, axis=1, keepdims=True) for s in ss])
    if first:
        m_new = jnp.broadcast_to(m_cur, (q.shape[0], LANE))
    else:
        m_prev = m_ref[idx]
        m_new = jnp.maximum(m_prev, m_cur)
    lsum, pv = None, None
    for s, (_, v, _) in zip(ss, kvs):
        p = jnp.exp(s - jnp.tile(m_new, (1, s.shape[1] // LANE)))
        ls = jnp.sum(p, axis=1, keepdims=True)
        d = _dot(p.astype(BF16), v)
        lsum = ls if lsum is None else lsum + ls
        pv = d if pv is None else pv + d
    if first:
        l_ref[idx] = jnp.broadcast_to(lsum, (q.shape[0], LANE))
        acc_ref[idx] = pv
    else:
        alpha = jnp.exp(m_prev - m_new)
        l_ref[idx] = alpha * l_ref[idx] + lsum
        acc_ref[idx] = alpha * acc_ref[idx] + pv
    m_ref[idx] = m_new


def _attn_kernel(qi_tab, kj_tab, qlo_ref, qhi_ref, k_ref, v_ref, km_ref, vm_ref, lam_ref, g_ref,
                 o_ref, m_ref, l_ref, acc_ref, *, n_heads, lam_init):
    s_idx = pl.program_id(1)
    qi = qi_tab[s_idx]
    kj = kj_tab[s_idx]
    tq = qlo_ref.shape[1]
    tk = k_ref.shape[1]

    def head_loop(kvs_of, first, finish=None):
        def body(h, c):
            kvs = kvs_of(h)
            _flash_update(2 * h, qlo_ref[h], kvs, m_ref, l_ref, acc_ref, first)
            _flash_update(2 * h + 1, qhi_ref[h], kvs, m_ref, l_ref, acc_ref, first)
            if finish is not None:
                finish(h)
            return c
        lax.fori_loop(0, n_heads, body, 0, unroll=4)

    @pl.when(jnp.logical_and(kj == 0, qi > 0))
    def _first():
        head_loop(lambda h: [(k_ref[h], v_ref[h], None)], True)

    @pl.when(jnp.logical_and(kj > 0, kj < qi))
    def _full():
        head_loop(lambda h: [(k_ref[h], v_ref[h], None)], False)

    @pl.when(kj == qi)
    def _diag():
        @pl.when(qi == 0)
        def _init():
            m_ref[...] = jnp.full(m_ref.shape, NEG, F32)
            l_ref[...] = jnp.zeros(l_ref.shape, F32)
            acc_ref[...] = jnp.zeros(acc_ref.shape, F32)

        row = lax.broadcasted_iota(jnp.int32, (tq, tk), 0)
        col = lax.broadcasted_iota(jnp.int32, (tq, tk), 1)
        causal = col <= row
        is_meta = lax.broadcasted_iota(jnp.int32, (tq, km_ref.shape[1]), 1) < N_META
        lam = lam_ref[...]
        g = g_ref[...] * (1.0 - lam_init)

        def finish(h):
            o = acc_ref[2 * h] / l_ref[2 * h] - lam * (acc_ref[2 * h + 1] / l_ref[2 * h + 1])
            o = o * lax.rsqrt(jnp.mean(o * o, axis=1, keepdims=True) + EPS) * g
            o_ref[h] = o.astype(BF16)

        head_loop(lambda h: [(k_ref[h], v_ref[h], causal), (km_ref[h], vm_ref[h], is_meta)], False, finish)


def _prompt_attention(qlo, qhi, kb, vb, kmeta, vmeta, lam_row, g_row, batch, lam_init, tile):
    n_heads, m, _ = qlo.shape
    seq = m // batch
    nq = seq // tile
    qi_list, kj_list = [], []
    for qi in range(nq):
        for kj in range(qi + 1):
            qi_list.append(qi)
            kj_list.append(kj)
    qi_tab = jnp.asarray(qi_list, jnp.int32)
    kj_tab = jnp.asarray(kj_list, jnp.int32)
    n_steps = len(qi_list)
    tmeta = kmeta.shape[1]
    qspec = pl.BlockSpec((n_heads, tile, LANE), lambda b, s, qt, kt: (0, b * nq + qt[s], 0))
    kspec = pl.BlockSpec((n_heads, tile, LANE), lambda b, s, qt, kt: (0, b * nq + kt[s], 0))
    mspec = pl.BlockSpec((n_heads, tmeta, LANE), lambda b, s, qt, kt: (0, 0, 0))
    rspec = pl.BlockSpec((1, LANE), lambda b, s, qt, kt: (0, 0))
    grid_spec = pltpu.PrefetchScalarGridSpec(
        num_scalar_prefetch=2,
        grid=(batch, n_steps),
        in_specs=[qspec, qspec, kspec, kspec, mspec, mspec, rspec, rspec],
        out_specs=qspec,
        scratch_shapes=[pltpu.VMEM((2 * n_heads, tile, LANE), F32),
                        pltpu.VMEM((2 * n_heads, tile, LANE), F32),
                        pltpu.VMEM((2 * n_heads, tile, LANE), F32)],
    )
    return pl.pallas_call(
        functools.partial(_attn_kernel, n_heads=n_heads, lam_init=lam_init),
        out_shape=jax.ShapeDtypeStruct((n_heads, m, LANE), BF16),
        grid_spec=grid_spec,
        compiler_params=_cparams("parallel", "arbitrary"),
        name="prompt_attention",
    )(qi_tab, kj_tab, qlo, qhi, kb, vb, kmeta, vmeta, lam_row, g_row)


def _paged_kernel(pt_ref, q_ref, *refs, n_heads, n_tok, pages, page, lam_init):
    k_refs = refs[:pages]
    v_refs = refs[pages:2 * pages]
    kn_ref, vn_ref, lam_ref, g_ref, o_ref, qbd_ref, kb_ref, vb_ref, m_ref, l_ref, acc_ref = refs[2 * pages:]
    j = pl.program_id(1)
    rows = 2 * n_heads * n_tok
    d = q_ref.shape[1]

    @pl.when(j == 0)
    def _init():
        q = q_ref[...]
        qt = jnp.tile(q, (rows // n_tok, 1))
        r = lax.broadcasted_iota(jnp.int32, (rows, d), 0)
        c = lax.broadcasted_iota(jnp.int32, (rows, d), 1)
        qbd_ref[...] = jnp.where((c // D_QK) == (r // n_tok), qt, 0.0).astype(BF16)
        m_ref[...] = jnp.full(m_ref.shape, NEG, F32)
        l_ref[...] = jnp.zeros(l_ref.shape, F32)
        acc_ref[...] = jnp.zeros(acc_ref.shape, F32)

    def update(k, v, mask):
        s = _dot_nt(qbd_ref[...], k)
        if mask is not None:
            s = jnp.where(mask, s, NEG)
        m_prev = m_ref[...]
        m_new = jnp.maximum(m_prev, jnp.max(s, axis=1, keepdims=True))
        alpha = jnp.exp(m_prev - m_new)
        p = jnp.exp(s - jnp.tile(m_new, (1, s.shape[1] // LANE)))
        l_ref[...] = alpha * l_ref[...] + jnp.sum(p, axis=1, keepdims=True)
        acc_ref[...] = jnp.tile(alpha, (1, d // LANE)) * acc_ref[...] + _dot(p.astype(BF16), v)
        m_ref[...] = m_new

    for p in range(pages):
        for h in range(n_heads):
            kb_ref[p * page:(p + 1) * page, h * LANE:(h + 1) * LANE] = (
                k_refs[p][pl.ds(h, page, stride=n_heads), :].astype(BF16))
            vb_ref[p * page:(p + 1) * page, h * LANE:(h + 1) * LANE] = (
                v_refs[p][pl.ds(h, page, stride=n_heads), :].astype(BF16))
    update(kb_ref[...], vb_ref[...], None)

    @pl.when(j == pl.num_programs(1) - 1)
    def _final():
        r = lax.broadcasted_iota(jnp.int32, (rows, kn_ref.shape[0]), 0)
        c = lax.broadcasted_iota(jnp.int32, (rows, kn_ref.shape[0]), 1)
        update(kn_ref[...], vn_ref[...], c <= (r % n_tok))
        o = acc_ref[...] / jnp.tile(l_ref[...], (1, d // LANE))
        lam = lam_ref[...]
        g = g_ref[...] * (1.0 - lam_init)
        for h in range(n_heads):
            r0 = 2 * h * n_tok
            o1 = o[r0:r0 + n_tok, h * LANE:(h + 1) * LANE]
            o2 = o[r0 + n_tok:r0 + 2 * n_tok, h * LANE:(h + 1) * LANE]
            oh = o1 - lam * o2
            oh = oh * lax.rsqrt(jnp.mean(oh * oh, axis=1, keepdims=True) + EPS) * g
            o_ref[:, h * LANE:(h + 1) * LANE] = oh


def _paged_attention(q, knew, vnew, cache_k, cache_v, page_table, lam_row, g_row, n_heads, lam_init, pages):
    bd, n_tok, d = q.shape
    page = cache_k.shape[1] // n_heads
    n_pages = page_table.shape[1]
    rows = 2 * n_heads * n_tok

    def page_spec(p):
        return pl.BlockSpec((None, page * n_heads, LANE),
                            lambda b, j, pt: (pt[b * n_pages + j * pages + p], 0, 0))

    bspec = pl.BlockSpec((None, n_tok, d), lambda b, j, pt: (b, 0, 0))
    nspec = pl.BlockSpec((None, knew.shape[1], d), lambda b, j, pt: (b, 0, 0))
    rspec = pl.BlockSpec((1, LANE), lambda b, j, pt: (0, 0))
    grid_spec = pltpu.PrefetchScalarGridSpec(
        num_scalar_prefetch=1,
        grid=(bd, n_pages // pages),
        in_specs=[bspec] + [page_spec(p) for p in range(pages)] * 2 + [nspec, nspec, rspec, rspec],
        out_specs=bspec,
        scratch_shapes=[pltpu.VMEM((rows, d), BF16),
                        pltpu.VMEM((pages * page, d), BF16),
                        pltpu.VMEM((pages * page, d), BF16),
                        pltpu.VMEM((rows, LANE), F32),
                        pltpu.VMEM((rows, LANE), F32),
                        pltpu.VMEM((rows, d), F32)],
    )
    args = [page_table.reshape(-1), q] + [cache_k] * pages + [cache_v] * pages + [knew, vnew, lam_row, g_row]
    return pl.pallas_call(
        functools.partial(_paged_kernel, n_heads=n_heads, n_tok=n_tok, pages=pages, page=page,
                          lam_init=lam_init),
        out_shape=jax.ShapeDtypeStruct((bd, n_tok, d), F32),
        grid_spec=grid_spec,
        compiler_params=_cparams("parallel", "arbitrary"),
        name="paged_attention",
    )(*args)


def _mlstm_kernel(qk_ref, v_ref, om_ref, gcol_ref, grow_ref, cinit_ref, c0_ref, n0_ref, m0_ref,
                  cw_ref, cb_ref, gn_ref, y_ref, c_ref, n_ref, m_ref, xpad_ref, *, n_heads):
    ci = pl.program_id(1)
    t = qk_ref.shape[0]
    dm = v_ref.shape[1]
    dh = dm // n_heads
    hist = CONV_W - 1

    @pl.when(ci == 0)
    def _init():
        xpad_ref[0:8, :] = cinit_ref[...]
        c_ref[...] = c0_ref[...]
        n_ref[...] = n0_ref[...]
        m_ref[...] = m0_ref[...]

    x = qk_ref[...]
    xpad_ref[8:8 + t, :] = x
    y = cb_ref[...] + cw_ref[hist:hist + 1, :] * x
    for j in range(hist):
        y = y + cw_ref[j:j + 1, :] * xpad_ref[8 - hist + j:8 - hist + j + t, :]
    xpad_ref[0:8, :] = x[t - 8:t, :]
    y = y * _sigmoid(y)
    q_all = y[:, :dm].astype(BF16)
    k_all = (y[:, dm:] * (dh ** -0.5)).astype(BF16)

    gcol = gcol_ref[...]
    grow = grow_ref[...]
    r = lax.broadcasted_iota(jnp.int32, (t, t), 0)
    c = lax.broadcasted_iota(jnp.int32, (t, t), 1)
    causal = c <= r
    tril = jnp.where(causal, 1.0, 0.0).astype(F32)
    triu = jnp.where(r <= c, 1.0, 0.0).astype(F32)
    bc_all = jnp.dot(tril, gcol, preferred_element_type=F32, precision=lax.Precision.HIGHEST)
    br_all = jnp.dot(grow, triu, preferred_element_type=F32, precision=lax.Precision.HIGHEST)

    for h in range(n_heads):
        sl = slice(h * dh, (h + 1) * dh)
        qh, kh, vh = q_all[:, sl], k_all[:, sl], v_ref[:, sl]
        m_prev = m_ref[h][:, 0:1]
        b_c = bc_all[:, n_heads + h:n_heads + h + 1]
        li_c = gcol[:, h:h + 1]
        b_r = br_all[n_heads + h:n_heads + h + 1, :]
        li_r = grow[h:h + 1, :]
        log_d = jnp.where(causal, b_c - b_r + li_r, NEG)
        m_t = jnp.maximum(b_c + m_prev, jnp.max(log_d, axis=1, keepdims=True))
        s = _dot_nt(qh, kh) * jnp.exp(log_d - m_t)
        w_inter = jnp.exp(b_c + m_prev - m_t)
        ch = c_ref[h]
        num = _dot(s.astype(BF16), vh) + w_inter * _dot_nt(qh, ch.astype(BF16))
        nh = n_ref[h]
        den = jnp.sum(s, axis=1, keepdims=True) + w_inter * jnp.sum(qh.astype(F32) * nh, axis=1, keepdims=True)
        hout = num / jnp.maximum(jnp.abs(den), jnp.exp(-m_t))
        hn = hout * lax.rsqrt(jnp.mean(hout * hout, axis=1, keepdims=True) + EPS) * gn_ref[:, sl]
        y_ref[:, sl] = (_sigmoid(om_ref[:, sl].astype(F32)) * hn).astype(BF16)
        b_last = b_r[:, t - 1:t]
        log_w_r = b_last - b_r + li_r
        m_new = jnp.maximum(b_last + m_prev, jnp.max(log_w_r, axis=1, keepdims=True))
        w_c = jnp.exp(b_last - b_c + li_c - m_new)
        decay = jnp.exp(b_last + m_prev - m_new)
        wv = (w_c * vh.astype(F32)).astype(BF16)
        c_ref[h] = decay * ch + _dot_tn(wv, kh)
        n_ref[h] = decay * nh + jnp.sum(w_c * kh.astype(F32), axis=0, keepdims=True)
        m_ref[h] = jnp.broadcast_to(m_new, (1, LANE))


def _mlstm(qk, v, om, gcol, grow, conv_init, c0, n0, m0, conv_w, conv_b, g_norm, chunk, shared_init):
    bs, seq, dm2 = qk.shape
    dm = dm2 // 2
    n_heads, dh = c0.shape[1], c0.shape[2]
    ng = gcol.shape[2]
    nc = seq // chunk
    sidx = (lambda b: 0) if shared_init else (lambda b: b)
    row = lambda w: pl.BlockSpec((None, chunk, w), lambda b, i: (b, i, 0))
    return pl.pallas_call(
        functools.partial(_mlstm_kernel, n_heads=n_heads),
        out_shape=(jax.ShapeDtypeStruct((bs, seq, dm), BF16),
                   jax.ShapeDtypeStruct((bs, n_heads, dh, dh), F32),
                   jax.ShapeDtypeStruct((bs, n_heads, 1, dh), F32),
                   jax.ShapeDtypeStruct((bs, n_heads, 1, LANE), F32)),
        grid=(bs, nc),
        in_specs=[row(dm2), row(dm), row(dm), row(ng),
                  pl.BlockSpec((None, ng, chunk), lambda b, i: (b, 0, i)),
                  pl.BlockSpec((None, 8, dm2), lambda b, i: (sidx(b), 0, 0)),
                  pl.BlockSpec((None, n_heads, dh, dh), lambda b, i: (sidx(b), 0, 0, 0)),
                  pl.BlockSpec((None, n_heads, 1, dh), lambda b, i: (sidx(b), 0, 0, 0)),
                  pl.BlockSpec((None, n_heads, 1, LANE), lambda b, i: (sidx(b), 0, 0, 0)),
                  pl.BlockSpec((CONV_W, dm2), lambda b, i: (0, 0)),
                  pl.BlockSpec((1, dm2), lambda b, i: (0, 0)),
                  pl.BlockSpec((1, dm), lambda b, i: (0, 0))],
        out_specs=(row(dm),
                   pl.BlockSpec((None, n_heads, dh, dh), lambda b, i: (b, 0, 0, 0)),
                   pl.BlockSpec((None, n_heads, 1, dh), lambda b, i: (b, 0, 0, 0)),
                   pl.BlockSpec((None, n_heads, 1, LANE), lambda b, i: (b, 0, 0, 0))),
        scratch_shapes=[pltpu.VMEM((chunk + 8, dm2), F32)],
        compiler_params=_cparams("parallel", "arbitrary"),
        name="mlstm",
    )(qk, v, om, gcol, grow, conv_init, c0, n0, m0, conv_w, conv_b, g_norm)


def _layer_norm(z, g, b):
    mu = jnp.mean(z, axis=1, keepdims=True)
    zc = z - mu
    var = jnp.mean(zc * zc, axis=1, keepdims=True)
    return zc * lax.rsqrt(var + EPS) * g + b


def _merge_kernel(x_ref, ya_ref, ym_ref, ga_ref, gm_ref, wba_ref, wbm_ref, wout_ref, g_ref, b_ref,
                  wrh_ref, wrl_ref, h_ref, hrow_ref, s_ref, *, head_major):
    tm, d = x_ref.shape
    if head_major:
        ya = jnp.concatenate([ya_ref[h] for h in range(ya_ref.shape[0])], axis=1)
    else:
        ya = ya_ref[...]
    a = _dot(ya, wba_ref[...])
    m = _dot(ym_ref[...], wbm_ref[...])
    mix = ga_ref[...].astype(F32) * a + gm_ref[...].astype(F32) * m
    z = ALPHA * x_ref[...] + _dot(mix.astype(BF16), wout_ref[...])
    h = _layer_norm(z, g_ref[...], b_ref[...])
    h_ref[...] = h
    n_chunk = d // LANE
    for c in range(n_chunk):
        hrow_ref[pl.ds(c, tm, stride=n_chunk), :] = h[:, c * LANE:(c + 1) * LANE]
    hh = h.astype(BF16)
    hl = (h - hh.astype(F32)).astype(BF16)
    wrh = wrh_ref[...]
    s_ref[...] = _sigmoid(_dot(hh, wrh) + _dot(hl, wrh) + _dot(hh, wrl_ref[...]))


def _merge(x, ya, ym, ga, gm, wba, wbm, wout, g, b, wrh, wrl, tm, head_major):
    m, d = x.shape
    ne = wrh.shape[1]
    row = pl.BlockSpec((tm, d), lambda i: (i, 0))
    full = lambda a: pl.BlockSpec(a.shape, lambda i: (0,) * a.ndim)
    ya_spec = pl.BlockSpec((ya.shape[0], tm, LANE), lambda i: (0, i, 0)) if head_major else row
    return pl.pallas_call(
        functools.partial(_merge_kernel, head_major=head_major),
        out_shape=(jax.ShapeDtypeStruct((m, d), F32), jax.ShapeDtypeStruct((m * (d // LANE), LANE), F32),
                   jax.ShapeDtypeStruct((m, ne), F32)),
        grid=(m // tm,),
        in_specs=[row, ya_spec, row, row, row, full(wba), full(wbm), full(wout), full(g), full(b),
                  full(wrh), full(wrl)],
        out_specs=(row, pl.BlockSpec((tm * (d // LANE), LANE), lambda i: (i, 0)),
                   pl.BlockSpec((tm, ne), lambda i: (i, 0))),
        compiler_params=_cparams("parallel"),
        name="merge_ln_router",
    )(x, ya, ym, ga, gm, wba, wbm, wout, g, b, wrh, wrl)


def _route_kernel(s_ref, b_ref, idx_ref, rank_ref, gate_ref, cnt_ref, carry_ref, *, n_exp):
    i = pl.program_id(0)

    @pl.when(i == 0)
    def _init():
        carry_ref[...] = jnp.zeros(carry_ref.shape, F32)

    s = s_ref[...]
    tm = s.shape[0]
    ssel = s + b_ref[...]
    gsz = n_exp // N_GROUPS
    ninf = -jnp.inf
    lane_i = lax.broadcasted_iota(jnp.int32, (tm, n_exp), 1)
    lane = lane_i.astype(F32)
    lgrp = lane_i // gsz
    slot = lax.broadcasted_iota(jnp.int32, (tm, LANE), 1)

    grp = []
    for g in range(N_GROUPS):
        mg = jnp.where(lgrp == g, ssel, ninf)
        m1 = jnp.max(mg, axis=1, keepdims=True)
        i1 = jnp.min(jnp.where(mg == m1, lane, float(n_exp)), axis=1, keepdims=True)
        m2 = jnp.max(jnp.where(lane == i1, ninf, mg), axis=1, keepdims=True)
        grp.append(m1 + m2)
    enabled = jnp.zeros((tm, n_exp), F32)
    for g in range(N_GROUPS):
        ahead = jnp.zeros((tm, 1), F32)
        for j in range(N_GROUPS):
            if j != g:
                beats = (grp[j] > grp[g]) if j > g else (grp[j] >= grp[g])
                ahead = ahead + jnp.where(beats, 1.0, 0.0)
        keep = jnp.where(ahead < TOPK_GROUPS, 1.0, 0.0)
        enabled = jnp.where(lgrp == g, keep, enabled)
    cur = jnp.where(enabled > 0.5, ssel, ninf)

    sel = jnp.zeros((tm, n_exp), F32)
    idx_out = jnp.zeros((tm, LANE), F32)
    gate_out = jnp.zeros((tm, LANE), F32)
    gsum = jnp.zeros((tm, 1), F32)
    picks = []
    for k in range(TOP_K):
        m = jnp.max(cur, axis=1, keepdims=True)
        ik = jnp.min(jnp.where(cur == m, lane, float(n_exp)), axis=1, keepdims=True)
        pick = lane == ik
        gk = jnp.sum(jnp.where(pick, s, 0.0), axis=1, keepdims=True)
        cur = jnp.where(pick, ninf, cur)
        sel = jnp.where(pick, 1.0, sel)
        idx_out = jnp.where(slot == k, ik, idx_out)
        gate_out = jnp.where(slot == k, gk, gate_out)
        gsum = gsum + gk
        picks.append(ik)
    idx_ref[...] = idx_out
    gate_ref[...] = gate_out / gsum * ROUTE_SCALE

    r = lax.broadcasted_iota(jnp.int32, (tm, tm), 0)
    c = lax.broadcasted_iota(jnp.int32, (tm, tm), 1)
    ltri = jnp.where(c < r, 1.0, 0.0).astype(BF16)
    rank_dense = carry_ref[...] + _dot(ltri, sel.astype(BF16))
    rank_out = jnp.zeros((tm, LANE), F32)
    for k in range(TOP_K):
        rk = jnp.sum(jnp.where(lane == picks[k], rank_dense, 0.0), axis=1, keepdims=True)
        rank_out = jnp.where(slot == k, rk, rank_out)
    rank_ref[...] = rank_out
    total = carry_ref[...] + jnp.sum(sel, axis=0, keepdims=True)
    carry_ref[...] = total
    cnt_ref[...] = jnp.broadcast_to(total, cnt_ref.shape)


def _route(s, b_row, tm):
    t, n_exp = s.shape
    tok = jax.ShapeDtypeStruct((t, LANE), F32)
    tspec = pl.BlockSpec((tm, LANE), lambda i: (i, 0))
    return pl.pallas_call(
        functools.partial(_route_kernel, n_exp=n_exp),
        out_shape=(tok, tok, tok, jax.ShapeDtypeStruct((8, n_exp), F32)),
        grid=(t // tm,),
        in_specs=[pl.BlockSpec((tm, n_exp), lambda i: (i, 0)), pl.BlockSpec((1, n_exp), lambda i: (0, 0))],
        out_specs=(tspec, tspec, tspec, pl.BlockSpec((8, n_exp), lambda i: (0, 0))),
        scratch_shapes=[pltpu.VMEM((1, n_exp), F32)],
        compiler_params=_cparams("arbitrary"),
        name="route",
    )(s, b_row)


def _row_copy(src_ref, s0, dst_ref, d0, rows, sem):
    return pltpu.make_async_copy(src_ref.at[pl.ds(pl.multiple_of(s0, 8), rows)],
                                 dst_ref.at[pl.ds(pl.multiple_of(d0, 8), rows)], sem)


RANK_BITS = 20


def _sorted_row(starts_ref, code, rows):
    return (starts_ref[code >> RANK_BITS] + (code & ((1 << RANK_BITS) - 1))) * rows


def _scatter_kernel(starts_ref, code_ref, src_ref, dst_ref, sem, *, n_tok, fan, rows):
    def issue(t, carry):
        for u in range(fan):
            d0 = _sorted_row(starts_ref, code_ref[0, t * fan + u], rows)
            _row_copy(src_ref, t * rows, dst_ref, d0, rows, sem).start()
        return carry
    lax.fori_loop(0, n_tok, issue, 0)

    def drain(t, carry):
        for u in range(fan):
            _row_copy(src_ref, 0, dst_ref, 0, rows, sem).wait()
        return carry
    lax.fori_loop(0, n_tok, drain, 0)


def _scatter_rows(src, starts, code, n_dst_rows, n_tok, fan, rows):
    steps = src.shape[0] // (n_tok * rows)
    grid_spec = pltpu.PrefetchScalarGridSpec(
        num_scalar_prefetch=1,
        grid=(steps,),
        in_specs=[pl.BlockSpec((None, 1, n_tok * fan), lambda i, st: (i, 0, 0), memory_space=pltpu.SMEM),
                  pl.BlockSpec((n_tok * rows, src.shape[1]), lambda i, st: (i, 0))],
        out_specs=pl.BlockSpec(memory_space=pl.ANY),
        scratch_shapes=[pltpu.SemaphoreType.DMA(())],
    )
    return pl.pallas_call(
        functools.partial(_scatter_kernel, n_tok=n_tok, fan=fan, rows=rows),
        out_shape=jax.ShapeDtypeStruct((n_dst_rows, src.shape[1]), src.dtype),
        grid_spec=grid_spec,
        compiler_params=_cparams("arbitrary"),
        name="scatter_rows",
    )(starts, code.reshape(steps, 1, n_tok * fan), src)


def _expert_kernel(win_ref, exp_ref, lo_ref, hi_ref, x_ref, w1_ref, w3_ref, w2_ref, y_ref,
                   w1b_ref, w3b_ref, w2b_ref, xb_ref, *, blk, n_chunk):
    s = pl.program_id(0)
    sp = jnp.maximum(s - 1, 0)
    new_win = jnp.logical_or(s == 0, win_ref[s] != win_ref[sp])
    new_exp = jnp.logical_or(s == 0, exp_ref[s] != exp_ref[sp])
    lo = lo_ref[s]
    hi = hi_ref[s]

    @pl.when(new_win)
    def _load_rows():
        for c in range(n_chunk):
            xb_ref[:, c * LANE:(c + 1) * LANE] = x_ref[pl.ds(c, blk, stride=n_chunk), :].astype(BF16)

    @pl.when(new_exp)
    def _cast():
        w1b_ref[...] = w1_ref[...].astype(BF16)
        w3b_ref[...] = w3_ref[...].astype(BF16)
        w2b_ref[...] = w2_ref[...].astype(BF16)

    def run(first_visit):
        x = xb_ref[...]
        a = _dot(x, w1b_ref[...])
        g = (a * _sigmoid(a)) * _dot(x, w3b_ref[...])
        y = _dot(g.astype(BF16), w2b_ref[...])
        row = lax.broadcasted_iota(jnp.int32, (blk, LANE), 0)
        mine = jnp.logical_and(row >= lo, row < hi)
        for c in range(n_chunk):
            sl = pl.ds(c, blk, stride=n_chunk)
            other = 0.0 if first_visit else y_ref[sl, :]
            y_ref[sl, :] = jnp.where(mine, y[:, c * LANE:(c + 1) * LANE], other)

    @pl.when(jnp.logical_and(hi > lo, new_win))
    def _run_first():
        run(True)

    @pl.when(jnp.logical_and(hi > lo, jnp.logical_not(new_win)))
    def _run_again():
        run(False)


def _experts(xrow, win, exp, lo, hi, w1, w3, w2, blk):
    d, de = w1.shape[1], w1.shape[2]
    n_chunk = d // LANE
    n_steps = win.shape[0]
    xspec = pl.BlockSpec((blk * n_chunk, LANE), lambda s, wn, ex, l, h: (wn[s], 0))
    grid_spec = pltpu.PrefetchScalarGridSpec(
        num_scalar_prefetch=4,
        grid=(n_steps,),
        in_specs=[xspec,
                  pl.BlockSpec((None, d, de), lambda s, wn, ex, l, h: (ex[s], 0, 0)),
                  pl.BlockSpec((None, d, de), lambda s, wn, ex, l, h: (ex[s], 0, 0)),
                  pl.BlockSpec((None, de, d), lambda s, wn, ex, l, h: (ex[s], 0, 0))],
        out_specs=xspec,
        scratch_shapes=[pltpu.VMEM((d, de), BF16), pltpu.VMEM((d, de), BF16), pltpu.VMEM((de, d), BF16),
                        pltpu.VMEM((blk, d), BF16)],
    )
    return pl.pallas_call(
        functools.partial(_expert_kernel, blk=blk, n_chunk=n_chunk),
        out_shape=jax.ShapeDtypeStruct(xrow.shape, F32),
        grid_spec=grid_spec,
        compiler_params=_cparams("arbitrary"),
        name="experts",
    )(win, exp, lo, hi, xrow, w1, w3, w2)


def _final_kernel(starts_ref, code_ref, code_next_ref, h_ref, gate_ref, yrow_ref, ws1_ref, ws3_ref, ws2_ref,
                  g_ref, b_ref, o_ref, ybuf_ref, sem, *, n_chunk):
    i = pl.program_id(0)
    slot = i % 2
    h = h_ref[...]
    tm = h.shape[0]

    def gather(cref, s):
        def issue(t, carry):
            for k in range(TOP_K):
                s0 = _sorted_row(starts_ref, cref[0, t * TOP_K + k], n_chunk)
                _row_copy(yrow_ref, s0, ybuf_ref.at[s], (k * tm + t) * n_chunk, n_chunk, sem.at[s]).start()
            return carry
        lax.fori_loop(0, tm, issue, 0)

    @pl.when(i == 0)
    def _prime():
        gather(code_ref, 0)

    @pl.when(i + 1 < pl.num_programs(0))
    def _prefetch():
        gather(code_next_ref, 1 - slot)

    yg_ref = ybuf_ref.at[slot]

    def drain(t, carry):
        for k in range(TOP_K):
            _row_copy(yrow_ref, 0, yg_ref, 0, n_chunk, sem.at[slot]).wait()
        return carry
    lax.fori_loop(0, tm, drain, 0)

    x = h.astype(BF16)
    a = _dot(x, ws1_ref[...])
    gg = (a * _sigmoid(a)) * _dot(x, ws3_ref[...])
    shared = _dot(gg.astype(BF16), ws2_ref[...])
    gate = gate_ref[...]
    gk = [jnp.broadcast_to(gate[:, k:k + 1], (tm, LANE)) for k in range(TOP_K)]
    chunks = []
    for c in range(n_chunk):
        acc = gk[0] * yg_ref[pl.ds(c, tm, stride=n_chunk), :]
        for k in range(1, TOP_K):
            acc = acc + gk[k] * yg_ref[pl.ds(k * tm * n_chunk + c, tm, stride=n_chunk), :]
        chunks.append(acc)
    f = jnp.concatenate(chunks, axis=1) + shared
    o_ref[...] = _layer_norm(ALPHA * h + f, g_ref[...], b_ref[...])


def _final(h, yrow, starts, code, gate, ws1, ws3, ws2, g, b, tm, blk_off):
    m, d = h.shape
    n_chunk = d // LANE
    n_blk = m // tm
    row = pl.BlockSpec((tm, d), lambda i, st: (i, 0))
    full = lambda a: pl.BlockSpec(a.shape, lambda i, st: (0,) * a.ndim)
    cspec = lambda nxt: pl.BlockSpec((None, 1, tm * TOP_K),
                                     lambda i, st: (jnp.minimum(i + nxt, n_blk - 1) + blk_off, 0, 0),
                                     memory_space=pltpu.SMEM)
    grid_spec = pltpu.PrefetchScalarGridSpec(
        num_scalar_prefetch=1,
        grid=(n_blk,),
        in_specs=[cspec(0), cspec(1), row, pl.BlockSpec((tm, LANE), lambda i, st: (i + blk_off, 0)),
                  pl.BlockSpec(memory_space=pl.ANY),
                  full(ws1), full(ws3), full(ws2), full(g), full(b)],
        out_specs=row,
        scratch_shapes=[pltpu.VMEM((2, tm * TOP_K * n_chunk, LANE), F32), pltpu.SemaphoreType.DMA((2,))],
    )
    code3 = code.reshape(-1, 1, tm * TOP_K)
    return pl.pallas_call(
        functools.partial(_final_kernel, n_chunk=n_chunk),
        out_shape=jax.ShapeDtypeStruct((m, d), F32),
        grid_spec=grid_spec,
        compiler_params=_cparams("arbitrary"),
        name="shared_ln",
    )(starts, code3, code3, h, gate, yrow, ws1, ws3, ws2, g, b)


def _rope_tables(pos):
    half = ROPE_DIM // 2
    inv = ROPE_THETA ** (-jnp.arange(half, dtype=F32) * 2.0 / ROPE_DIM)
    ang = pos.astype(F32)[:, None] * inv
    cos8, sin8 = jnp.cos(ang), jnp.sin(ang)
    n = pos.shape[0]
    cos = jnp.concatenate([cos8, cos8, jnp.ones((n, D_QK - ROPE_DIM), F32)], axis=1)
    sin = jnp.concatenate([-sin8, sin8, jnp.zeros((n, D_QK - ROPE_DIM), F32)], axis=1)
    return jnp.tile(cos, (1, LANE // D_QK)), jnp.tile(sin, (1, LANE // D_QK))


def _expert_steps(starts, ends, n_win, blk):
    n_exp = starts.shape[0]
    first_w = starts // blk
    n_w = jnp.where(ends > starts, (ends + blk - 1) // blk - first_w, 0)
    step_end = jnp.cumsum(n_w)
    step_start = step_end - n_w
    total = step_end[-1]
    s = jnp.arange(n_win + n_exp, dtype=jnp.int32)
    e = jnp.minimum(jnp.searchsorted(step_end, s, side='right'), n_exp - 1).astype(jnp.int32)
    w = first_w[e] + (s - step_start[e])
    lo = jnp.maximum(starts[e], w * blk) - w * blk
    hi = jnp.minimum(ends[e], (w + 1) * blk) - w * blk
    valid = s < total
    last = jnp.maximum(total - 1, 0)
    e = jnp.where(valid, e, e[last])
    w = jnp.where(valid, w, w[last])
    lo = jnp.where(valid, lo, 0)
    hi = jnp.where(valid, hi, 0)
    return w.astype(jnp.int32), e, lo.astype(jnp.int32), hi.astype(jnp.int32)


def _row_tile(m, pref):
    t = pref
    while m % t:
        t //= 2
    return t


def kernel(x_prompt, x_sample, cache_k, cache_v, page_table, state_C, state_n, state_m, state_conv, meta_tokens, w_in, b_in, lam_q1, lam_k1, lam_q2, lam_k2, g_subln, conv_w, conv_b, g_mnorm, w_ba, w_bm, w_out, ln1_g, ln1_b, w_router, b_router, w1, w3, w2, ws1, ws3, ws2, ln2_g, ln2_b):
    B, SEQ, D = x_prompt.shape
    BD, S, _ = x_sample.shape
    NHA, DVA = cache_k.shape[3], cache_v.shape[4]
    NHM, DHM = state_C.shape[2], state_C.shape[3]
    n_exp = w_router.shape[2]
    n_pages = page_table.shape[1]
    page = cache_k.shape[2]
    l = 0
    lam_init = 0.8 - 0.6 * math.exp(-0.3 * l)
    lam = (jnp.exp(jnp.sum(lam_q1[l] * lam_k1[l])) - jnp.exp(jnp.sum(lam_q2[l] * lam_k2[l])) + lam_init)
    lam_row = jnp.broadcast_to(lam.astype(F32), (1, LANE))
    g_row = g_subln[l].astype(F32).reshape(1, LANE)

    w = w_in[l]
    bias = b_in[l].astype(F32)
    o_mix = 3 * D
    o_if = o_mix + 4 * D
    o_g = o_if + 2 * NHM
    w_qkv = w[:, :o_mix].astype(BF16)
    w_mix = w[:, o_mix:o_if].astype(BF16)
    w_g = w[:, o_g:].astype(BF16)
    w_if = w[:, o_if:o_g]
    b_qkv = bias[None, :o_mix]
    b_mix = bias[None, o_mix:o_if]
    b_g = bias[None, o_g:]
    b_if = bias[o_if:o_g]

    MP = B * SEQ
    xp = x_prompt.reshape(MP, D)
    xs = x_sample.reshape(BD * S, D)
    xm = jnp.zeros((PAD_CHUNK, D), F32).at[:N_META].set(meta_tokens.astype(F32))
    tmp = _row_tile(SEQ, 512)
    tms = _row_tile(BD * S, 256)

    cos_p, sin_p = _rope_tables(N_META + jnp.arange(SEQ))
    cos_s, sin_s = _rope_tables(jnp.tile(n_pages * page + jnp.arange(S), BD))
    cos_m, sin_m = _rope_tables(jnp.arange(PAD_CHUNK))

    xp_bf, xs_bf, xm_bf = xp.astype(BF16), xs.astype(BF16), xm.astype(BF16)
    qlo_p, qhi_p, kf_p, kb_p, vf_p, vb_p = _proj_qkv(xp_bf, w_qkv, b_qkv, cos_p, sin_p, tmp, NHA)
    qlo_s, qhi_s, kf_s, _, vf_s, _ = _proj_qkv(xs_bf, w_qkv, b_qkv, cos_s, sin_s, tms, NHA)
    _, _, kf_m, kb_m, vf_m, vb_m = _proj_qkv(xm_bf, w_qkv, b_qkv, cos_m, sin_m, PAD_CHUNK, NHA)

    qk_p, vm_p, om_p = _proj_mix(xp_bf, w_mix, b_mix, tmp)
    qk_s, vm_s, om_s = _proj_mix(xs_bf, w_mix, b_mix, tms)
    qk_m, vm_m, om_m = _proj_mix(xm_bf, w_mix, b_mix, PAD_CHUNK)

    ga_p, gm_p = _proj_gates(xp_bf, w_g, b_g, tmp)
    ga_s, gm_s = _proj_gates(xs_bf, w_g, b_g, tms)

    if_p = _proj_if(xp, w_if, b_if, tmp, NHM)[:, :2 * NHM]
    if_s = _proj_if(xs, w_if, b_if, tms, NHM)[:, :2 * NHM]
    if_m = _proj_if(xm, w_if, b_if, PAD_CHUNK, NHM)[:, :2 * NHM]

    ya_p = _prompt_attention(qlo_p, qhi_p, kb_p, vb_p, kb_m, vb_m, lam_row, g_row, B, lam_init,
                             _row_tile(SEQ, ATTN_TILE))
    q_s = jnp.transpose(qlo_s + qhi_s, (1, 0, 2)).reshape(BD, S, D).astype(F32)
    kn = jnp.zeros((BD, LANE, D), BF16).at[:, :S].set(kf_s.reshape(BD, S, D).astype(BF16))
    vn = jnp.zeros((BD, LANE, D), BF16).at[:, :S].set(vf_s.reshape(BD, S, D).astype(BF16))
    n_pool = cache_k.shape[1]
    ya_s = _paged_attention(q_s, kn, vn, cache_k.reshape(-1, page * NHA, DVA), cache_v.reshape(-1, page * NHA, DVA),
                            page_table + l * n_pool, lam_row, g_row, NHA, lam_init,
                            math.gcd(PAGES_PER_STEP, n_pages))

    cw = conv_w[l].astype(F32)
    cb = conv_b[l].astype(F32)[None]
    gn = g_mnorm[l].astype(F32)[None]
    pad_gate = jnp.concatenate([jnp.full((NHM,), NEG, F32), jnp.zeros((NHM,), F32)])

    def gate_layouts(g, bs, real, total):
        g = g.reshape(bs, -1, 2 * NHM)[:, :real]
        if total > real:
            g = jnp.concatenate([g, jnp.broadcast_to(pad_gate, (bs, total - real, 2 * NHM))], axis=1)
        return g, jnp.transpose(g, (0, 2, 1))

    def pad_rows(a, bs, real, total):
        a = a.reshape(bs, -1, a.shape[-1])[:, :real]
        return jnp.pad(a, ((0, 0), (0, total - real), (0, 0)))

    gcol_m, grow_m = gate_layouts(if_m, 1, N_META, PAD_CHUNK)
    zC = jnp.zeros((1, NHM, DHM, DHM), F32)
    zn = jnp.zeros((1, NHM, 1, DHM), F32)
    zm = jnp.zeros((1, NHM, 1, LANE), F32)
    _, c_m, n_m, m_m = _mlstm(pad_rows(qk_m, 1, N_META, PAD_CHUNK), pad_rows(vm_m, 1, N_META, PAD_CHUNK),
                              pad_rows(om_m, 1, N_META, PAD_CHUNK), gcol_m, grow_m,
                              jnp.zeros((1, 8, 2 * D), F32), zC, zn, zm, cw, cb, gn, PAD_CHUNK, True)
    hist_m = jnp.zeros((1, 8, 2 * D), F32).at[0, 8 - (CONV_W - 1):].set(qk_m[N_META - (CONV_W - 1):N_META])
    gcol_p, grow_p = gate_layouts(if_p, B, SEQ, SEQ)
    ym_p, c_p, n_p, m_p = _mlstm(qk_p.reshape(B, SEQ, 2 * D), vm_p.reshape(B, SEQ, D), om_p.reshape(B, SEQ, D),
                                 gcol_p, grow_p, hist_m, c_m, n_m, m_m, cw, cb, gn,
                                 _row_tile(SEQ, MLSTM_CHUNK), True)
    gcol_s, grow_s = gate_layouts(if_s, BD, S, PAD_CHUNK)
    hist_s = jnp.zeros((BD, 8, 2 * D), F32).at[:, 8 - (CONV_W - 1):].set(state_conv[l].astype(F32))
    ym_s, c_s, n_s, m_s = _mlstm(pad_rows(qk_s, BD, S, PAD_CHUNK), pad_rows(vm_s, BD, S, PAD_CHUNK),
                                 pad_rows(om_s, BD, S, PAD_CHUNK), gcol_s, grow_s, hist_s,
                                 state_C[l].astype(F32), state_n[l].astype(F32)[:, :, None, :],
                                 jnp.broadcast_to(state_m[l].astype(F32)[:, :, None, None], (BD, NHM, 1, LANE)),
                                 cw, cb, gn, PAD_CHUNK, False)
    ym_s = ym_s[:, :S].reshape(BD * S, D)

    wba, wbm, wout = w_ba[l].astype(BF16), w_bm[l].astype(BF16), w_out[l].astype(BF16)
    g1, b1 = ln1_g[l].astype(F32)[None], ln1_b[l].astype(F32)[None]
    wr = w_router[l].astype(F32)
    wrh = wr.astype(BF16)
    wrl = (wr - wrh.astype(F32)).astype(BF16)
    h_p, hrow_p, s_p = _merge(xp, ya_p, ym_p.reshape(MP, D), ga_p, gm_p, wba, wbm, wout, g1, b1, wrh, wrl,
                              _row_tile(MP, 256), True)
    h_s, hrow_s, s_s = _merge(xs, ya_s.reshape(BD * S, D).astype(BF16), ym_s, ga_s, gm_s, wba, wbm, wout,
                              g1, b1, wrh, wrl, tms, False)

    MT = MP + BD * S
    n_chunk = D // LANE
    tk = MT * TOP_K
    tt = _row_tile(math.gcd(MP, BD * S), 256)
    hrow = jnp.concatenate([hrow_p, hrow_s], axis=0)
    s_all = jnp.concatenate([s_p, s_s], axis=0)
    idx_f, rank_f, gate, cnt = _route(s_all, b_router[l].astype(F32)[None], tt)
    idx = idx_f[:, :TOP_K].astype(jnp.int32)
    counts = cnt[0].astype(jnp.int32)
    ends = jnp.cumsum(counts)
    starts = ends - counts
    assert MT < (1 << RANK_BITS)
    code = ((idx << RANK_BITS) | rank_f[:, :TOP_K].astype(jnp.int32)).reshape(tk)
    n_win = -(-tk // MOE_BLOCK)
    xrow = _scatter_rows(hrow, starts, code, n_win * MOE_BLOCK * n_chunk, tt, TOP_K, n_chunk)
    win, exp, lo, hi = _expert_steps(starts, ends, n_win, MOE_BLOCK)
    yrow = _experts(xrow, win, exp, lo, hi, w1[l], w3[l], w2[l], MOE_BLOCK)

    ws1b, ws3b, ws2b = ws1[l].astype(BF16), ws3[l].astype(BF16), ws2[l].astype(BF16)
    g2, b2 = ln2_g[l].astype(F32)[None], ln2_b[l].astype(F32)[None]
    y_p = _final(h_p, yrow, starts, code, gate, ws1b, ws3b, ws2b, g2, b2, tt, 0)
    y_s = _final(h_s, yrow, starts, code, gate, ws1b, ws3b, ws2b, g2, b2, tt, MP // tt)

    def with_meta(meta_rows, real):
        meta = jnp.broadcast_to(meta_rows[None, :N_META], (B, N_META, D))
        return jnp.concatenate([meta, real.reshape(B, SEQ, D)], axis=1).reshape(1, B, N_META + SEQ, NHA, DVA)

    k_prompt = with_meta(kf_m, kf_p)
    v_prompt = with_meta(vf_m, vf_p)
    conv_prompt = qk_p.reshape(B, SEQ, 2 * D)[:, SEQ - (CONV_W - 1):][None]
    conv_sample = qk_s.reshape(BD, S, 2 * D)[:, S - (CONV_W - 1):][None]
    return (y_p.reshape(B, SEQ, D), y_s.reshape(BD, S, D),
            k_prompt, v_prompt,
            c_p[None], n_p[:, :, 0][None], m_p[:, :, 0, 0][None], conv_prompt,
            kf_s.reshape(1, BD, S, NHA, DVA), vf_s.reshape(1, BD, S, NHA, DVA),
            c_s[None], n_s[:, :, 0][None], m_s[:, :, 0, 0][None], conv_sample)
```
